```python
import jax, jax.numpy as jnp
from jax import lax
import numpy as np

D_MODEL = 1024
BATCH = 8
SEQ = 4096
DEPTH = 1

CHUNK = 64
N_META = 16
NORM_EPS = 1e-5
D_FF = 2816

RWKV_HEAD = 64
RWKV_HEADS = 8
RWKV_WIDTH = RWKV_HEADS * RWKV_HEAD
LORA_W = 32
LORA_A = 32
LORA_G = 96
GN_EPS = 64e-5

ATTN_HEAD = 64
ATTN_HEADS = 8
ATTN_KV_HEADS = 2
ATTN_GROUP = ATTN_HEADS // ATTN_KV_HEADS
ATTN_WIDTH = ATTN_HEADS * ATTN_HEAD
WINDOW = 128
WIN_CHUNKS = WINDOW // CHUNK
NEG_INF = -1e30

MIX_WIDTH = RWKV_WIDTH + ATTN_WIDTH
RWKV_COLS = 3 * RWKV_WIDTH + LORA_W + LORA_A + LORA_G
ATTN_COLS = (ATTN_HEADS + 2 * ATTN_KV_HEADS) * ATTN_HEAD
IN_COLS = RWKV_COLS + ATTN_COLS

kernel_name = "hymba_rwkv7_swa_sink_macaron"


def rmsnorm(x, g):
    xf = x.astype(jnp.float32)
    y = xf * lax.rsqrt(jnp.mean(xf * xf, axis=-1, keepdims=True) + NORM_EPS)
    return (y * g.astype(jnp.float32)).astype(x.dtype)


def swiglu(x, w_gate, w_up, w_down):
    return (jax.nn.silu(x @ w_gate) * (x @ w_up)) @ w_down


def token_shift(z):
    return jnp.pad(z, ((0, 0), (1, 0), (0, 0)))[:, :-1]


def alibi_slopes(n_heads):
    return jnp.exp2(-8.0 * (jnp.arange(n_heads, dtype=jnp.float32) + 1.0) / n_heads)


def rwkv7_mix(z, mu, w0, w2, a0, a2, g2, k_k, k_a, r_k, ln_w, ln_b):
    f32 = jnp.float32
    B, L, _ = z.shape
    zf = z.astype(f32)
    zf = zf + (token_shift(zf) - zf) * mu.astype(f32)
    s1 = RWKV_WIDTH
    s2 = 2 * RWKV_WIDTH
    s3 = 3 * RWKV_WIDTH
    s4 = s3 + LORA_W
    s5 = s4 + LORA_A
    zr, zk, zv, zw, za, zg = jnp.split(zf, [s1, s2, s3, s4, s5], axis=-1)
    w_log = -jax.nn.softplus(-(w0.astype(f32) + jnp.tanh(zw) @ w2.astype(f32))) - 0.5
    decay = jnp.exp(-jnp.exp(w_log))
    a = jax.nn.sigmoid(a0.astype(f32) + za @ a2.astype(f32))
    g = jax.nn.sigmoid(zg) @ g2.astype(f32)
    hs = lambda t: t.reshape(B, L, RWKV_HEADS, RWKV_HEAD)
    kk = hs(zk * k_k.astype(f32))
    kk = kk / jnp.maximum(jnp.sqrt(jnp.sum(kk * kk, axis=-1, keepdims=True)), 1e-12)
    k = zk * (1.0 + (a - 1.0) * k_a.astype(f32))
    r_h, k_h, v_h, a_h, w_h = hs(zr), hs(k), hs(zv), hs(a), hs(decay)
    tm = lambda t: jnp.transpose(t, (1, 0, 2, 3))

    def step(S, inp):
        r_t, w_t, k_t, v_t, kk_t, a_t = inp
        sa = jnp.einsum('bhvk,bhk->bhv', S, -kk_t)
        S = (S * w_t[:, :, None, :]
             + sa[..., None] * (kk_t * a_t)[:, :, None, :]
             + v_t[..., None] * k_t[:, :, None, :])
        y_t = jnp.einsum('bhvk,bhk->bhv', S, r_t)
        return S, y_t

    S0 = jnp.zeros((B, RWKV_HEADS, RWKV_HEAD, RWKV_HEAD), f32)
    _, y = lax.scan(step, S0, (tm(r_h), tm(w_h), tm(k_h), tm(v_h), tm(kk), tm(a_h)))
    y = jnp.transpose(y, (1, 0, 2, 3))
    mean = jnp.mean(y, axis=-1, keepdims=True)
    var = jnp.mean(jnp.square(y - mean), axis=-1, keepdims=True)
    y = ((y - mean) * lax.rsqrt(var + GN_EPS)).reshape(B, L, RWKV_WIDTH)
    y = y * ln_w.astype(f32) + ln_b.astype(f32)
    bonus = jnp.sum(r_h * k_h * r_k.astype(f32), axis=-1, keepdims=True) * v_h
    out = (y + bonus.reshape(B, L, RWKV_WIDTH)) * g
    return out.astype(z.dtype)


def swa_sink_attention(q, k, v, sinks):
    f32 = jnp.float32
    B, L = q.shape[0], q.shape[1]
    n_real = L - N_META
    nc = n_real // CHUNK
    scale = ATTN_HEAD ** -0.5
    slopes = alibi_slopes(ATTN_HEADS).reshape(ATTN_KV_HEADS, ATTN_GROUP)
    sink = sinks.astype(f32).reshape(ATTN_KV_HEADS, ATTN_GROUP)
    q = q.reshape(B, L, ATTN_KV_HEADS, ATTN_GROUP, ATTN_HEAD)
    qm = q[:, :N_META]
    qr = q[:, N_META:].reshape(B, nc, CHUNK, ATTN_KV_HEADS, ATTN_GROUP, ATTN_HEAD)
    km, vm = k[:, :N_META], v[:, :N_META]
    kr = k[:, N_META:].reshape(B, nc, CHUNK, ATTN_KV_HEADS, ATTN_HEAD)
    vr = v[:, N_META:].reshape(B, nc, CHUNK, ATTN_KV_HEADS, ATTN_HEAD)
    pad = ((0, 0), (WIN_CHUNKS, 0), (0, 0), (0, 0), (0, 0))
    kp, vp = jnp.pad(kr, pad), jnp.pad(vr, pad)
    kw = jnp.concatenate([kp[:, i:i + nc] for i in range(WIN_CHUNKS + 1)], axis=2)
    vw = jnp.concatenate([vp[:, i:i + nc] for i in range(WIN_CHUNKS + 1)], axis=2)
    n_wk = (WIN_CHUNKS + 1) * CHUNK
    c = jnp.arange(nc)
    q_pos = N_META + c[:, None] * CHUNK + jnp.arange(CHUNK)[None, :]
    k_pos = N_META + (c[:, None] - WIN_CHUNKS) * CHUNK + jnp.arange(n_wk)[None, :]
    k_valid = k_pos >= N_META
    m_pos = jnp.arange(N_META)
    dist_w = jnp.abs(q_pos[:, :, None] - k_pos[:, None, :]).astype(f32)
    dist_m = jnp.abs(q_pos[:, :, None] - m_pos[None, None, :]).astype(f32)
    sl5 = slopes[:, :, None, None, None]
    s_w = jnp.einsum('bcqkgd,bcskd->bkgcqs', qr, kw).astype(f32) * scale - sl5 * dist_w
    s_w = jnp.where(k_valid[:, None, :], s_w, NEG_INF)
    s_m = jnp.einsum('bcqkgd,bmkd->bkgcqm', qr, km).astype(f32) * scale - sl5 * dist_m
    s_sink = jnp.broadcast_to(sink[:, :, None, None, None], s_m.shape[:-1] + (1,))
    p = jax.nn.softmax(jnp.concatenate([s_sink, s_m, s_w], axis=-1), axis=-1)
    p_m = p[..., 1:1 + N_META].astype(v.dtype)
    p_w = p[..., 1 + N_META:].astype(v.dtype)
    o_r = (jnp.einsum('bkgcqm,bmkd->bcqkgd', p_m, vm)
           + jnp.einsum('bkgcqs,bcskd->bcqkgd', p_w, vw)).reshape(B, n_real, ATTN_WIDTH)
    dist_mm = jnp.abs(m_pos[:, None] - m_pos[None, :]).astype(f32)
    s_mm = jnp.einsum('bqkgd,bmkd->bkgqm', qm, km).astype(f32) * scale - slopes[:, :, None, None] * dist_mm
    s_msink = jnp.broadcast_to(sink[:, :, None, None], s_mm.shape[:-1] + (1,))
    pm = jax.nn.softmax(jnp.concatenate([s_msink, s_mm], axis=-1), axis=-1)[..., 1:].astype(v.dtype)
    o_m = jnp.einsum('bkgqm,bmkd->bqkgd', pm, vm).reshape(B, N_META, ATTN_WIDTH)
    return jnp.concatenate([o_m, o_r], axis=1)


def setup_inputs(seed: int = 0) -> dict:
    key = jax.random.key(seed)
    ks = iter(jax.random.split(key, 40))
    f32 = jnp.float32

    def nrm(shape, scale):
        return jax.random.normal(next(ks), shape, f32) * scale

    def gain(shape):
        return 1.0 + 0.02 * jax.random.normal(next(ks), shape, f32)

    Ld = DEPTH
    return {
        "x": nrm((BATCH, SEQ, D_MODEL), 1.0),
        "meta_tokens": nrm((N_META, D_MODEL), 1.0),
        "ffn1_norm": gain((Ld, D_MODEL)),
        "ffn1_w_gate": nrm((Ld, D_MODEL, D_FF), D_MODEL ** -0.5),
        "ffn1_w_up": nrm((Ld, D_MODEL, D_FF), D_MODEL ** -0.5),
        "ffn1_w_down": nrm((Ld, D_FF, D_MODEL), D_FF ** -0.5),
        "mix_norm": gain((Ld, D_MODEL)),
        "w_in": nrm((Ld, D_MODEL, IN_COLS), D_MODEL ** -0.5),
        "b_attn": nrm((Ld, ATTN_COLS), 0.02),
        "rwkv_mu": jax.random.uniform(next(ks), (Ld, RWKV_COLS), f32),
        "rwkv_w0": jax.random.uniform(next(ks), (Ld, RWKV_WIDTH), f32, -6.0, 1.0),
        "rwkv_w2": nrm((Ld, LORA_W, RWKV_WIDTH), 0.1),
        "rwkv_a0": nrm((Ld, RWKV_WIDTH), 0.1),
        "rwkv_a2": nrm((Ld, LORA_A, RWKV_WIDTH), 0.1),
        "rwkv_g2": nrm((Ld, LORA_G, RWKV_WIDTH), LORA_G ** -0.5),
        "rwkv_k_k": 0.85 + 0.02 * jax.random.normal(next(ks), (Ld, RWKV_WIDTH), f32),
        "rwkv_k_a": gain((Ld, RWKV_WIDTH)),
        "rwkv_r_k": nrm((Ld, RWKV_HEADS, RWKV_HEAD), 0.1),
        "rwkv_ln_w": gain((Ld, RWKV_WIDTH)),
        "rwkv_ln_b": nrm((Ld, RWKV_WIDTH), 0.02),
        "attn_sinks": nrm((Ld, ATTN_HEADS), 0.5),
        "w_out": nrm((Ld, MIX_WIDTH, D_MODEL), MIX_WIDTH ** -0.5),
        "ffn2_norm": gain((Ld, D_MODEL)),
        "ffn2_w_gate": nrm((Ld, D_MODEL, D_FF), D_MODEL ** -0.5),
        "ffn2_w_up": nrm((Ld, D_MODEL, D_FF), D_MODEL ** -0.5),
        "ffn2_w_down": nrm((Ld, D_FF, D_MODEL), D_FF ** -0.5),
        "final_norm": gain((D_MODEL,)),
    }


def reference(x, meta_tokens, ffn1_norm, ffn1_w_gate, ffn1_w_up, ffn1_w_down, mix_norm, w_in, b_attn,
              rwkv_mu, rwkv_w0, rwkv_w2, rwkv_a0, rwkv_a2, rwkv_g2, rwkv_k_k, rwkv_k_a, rwkv_r_k,
              rwkv_ln_w, rwkv_ln_b, attn_sinks, w_out, ffn2_norm, ffn2_w_gate, ffn2_w_up, ffn2_w_down,
              final_norm):
    B = x.shape[0]
    meta = jnp.broadcast_to(meta_tokens[None].astype(x.dtype), (B, N_META, D_MODEL))
    h = jnp.concatenate([meta, x], axis=1)
    L = h.shape[1]
    for l in range(DEPTH):
        h = h + 0.5 * swiglu(rmsnorm(h, ffn1_norm[l]), ffn1_w_gate[l], ffn1_w_up[l], ffn1_w_down[l])
        z = rmsnorm(h, mix_norm[l]) @ w_in[l]
        z_rwkv = z[..., :RWKV_COLS]
        z_attn = z[..., RWKV_COLS:] + b_attn[l]
        y_rwkv = rwkv7_mix(z_rwkv, rwkv_mu[l], rwkv_w0[l], rwkv_w2[l], rwkv_a0[l], rwkv_a2[l], rwkv_g2[l],
                           rwkv_k_k[l], rwkv_k_a[l], rwkv_r_k[l], rwkv_ln_w[l], rwkv_ln_b[l])
        kv_w = ATTN_KV_HEADS * ATTN_HEAD
        q = z_attn[..., :ATTN_WIDTH].reshape(B, L, ATTN_HEADS, ATTN_HEAD)
        k = z_attn[..., ATTN_WIDTH:ATTN_WIDTH + kv_w].reshape(B, L, ATTN_KV_HEADS, ATTN_HEAD)
        v = z_attn[..., ATTN_WIDTH + kv_w:].reshape(B, L, ATTN_KV_HEADS, ATTN_HEAD)
        y_attn = swa_sink_attention(q, k, v, attn_sinks[l])
        h = h + jnp.concatenate([y_rwkv, y_attn], axis=-1) @ w_out[l]
        h = h + 0.5 * swiglu(rmsnorm(h, ffn2_norm[l]), ffn2_w_gate[l], ffn2_w_up[l], ffn2_w_down[l])
    return rmsnorm(h, final_norm)[:, N_META:]
```

```python
import functools

import jax
import jax.numpy as jnp
from jax import lax
from jax.experimental import pallas as pl
from jax.experimental.pallas import tpu as pltpu

F32 = jnp.float32
BF16 = jnp.bfloat16
HI = lax.Precision.HIGHEST

D_MODEL = 1024
D_FF = 2816
N_META = 16
NORM_EPS = 1e-5
CHUNK = 64

HEAD = 64
RWKV_HEADS = 8
RWKV_WIDTH = 512
LORA_W, LORA_A, LORA_G = 32, 32, 96
LORA_COLS = LORA_W + LORA_A + LORA_G
LORA_PAD = 256
GN_EPS = 64e-5

ATTN_HEADS = 8
ATTN_KV_HEADS = 2
ATTN_GROUP = 4
ATTN_WIDTH = 512
KV_WIDTH = 2 * ATTN_KV_HEADS * HEAD
WIN_CHUNKS = 2
NEG_INF = -1e30

RKV_COLS = 3 * RWKV_WIDTH
Z_COLS = RKV_COLS + LORA_PAD + ATTN_WIDTH + KV_WIDTH

FF_CHUNK = 256
VMEM_LIMIT = 56 * 1024 * 1024


def _const_spec(shape):
    nd = len(shape)
    return pl.BlockSpec(shape, lambda *_: (0,) * nd, pipeline_mode=pl.Buffered(1))


def _rmsnorm(x, g):
    return x * lax.rsqrt(jnp.mean(x * x, axis=-1, keepdims=True) + NORM_EPS) * g


def _swiglu(n, wg_ref, wu_ref, wd_ref):
    acc = jnp.zeros((n.shape[0], D_MODEL), F32)
    for c in range(D_FF // FF_CHUNK):
        sl = slice(c * FF_CHUNK, (c + 1) * FF_CHUNK)
        g = jnp.dot(n, wg_ref[:, sl], preferred_element_type=F32)
        u = jnp.dot(n, wu_ref[:, sl], preferred_element_type=F32)
        a = (g * jax.nn.sigmoid(g) * u).astype(BF16)
        acc = acc + jnp.dot(a, wd_ref[sl, :], preferred_element_type=F32)
    return acc


def _ffn_in_kernel(x_ref, g1_ref, wg_ref, wu_ref, wd_ref, g2_ref, win_ref, bin_ref,
                   h_ref, zrkv_ref, zlora_ref, q_ref, kv_ref):
    x = x_ref[...]
    n1 = _rmsnorm(x, g1_ref[...]).astype(BF16)
    h = x + 0.5 * _swiglu(n1, wg_ref, wu_ref, wd_ref)
    h_ref[...] = h
    n2 = _rmsnorm(h, g2_ref[...]).astype(BF16)
    z = jnp.dot(n2, win_ref[...], preferred_element_type=F32) + bin_ref[...]
    o = 0
    for ref, w in ((zrkv_ref, RKV_COLS), (zlora_ref, LORA_PAD), (q_ref, ATTN_WIDTH), (kv_ref, KV_WIDTH)):
        ref[...] = z[:, o:o + w]
        o += w


def _ffn_in(x, g1, wg, wu, wd, g2, win, b_in, tm):
    rows = x.shape[0]
    row = lambda w: pl.BlockSpec((tm, w), lambda i: (i, 0))
    return pl.pallas_call(
        _ffn_in_kernel,
        grid=(rows // tm,),
        in_specs=[row(D_MODEL), _const_spec((1, D_MODEL)), _const_spec((D_MODEL, D_FF)),
                  _const_spec((D_MODEL, D_FF)), _const_spec((D_FF, D_MODEL)), _const_spec((1, D_MODEL)),
                  _const_spec((D_MODEL, Z_COLS)), _const_spec((1, Z_COLS))],
        out_specs=[row(D_MODEL), row(RKV_COLS), row(LORA_PAD), row(ATTN_WIDTH), row(KV_WIDTH)],
        out_shape=[jax.ShapeDtypeStruct((rows, w), F32)
                   for w in (D_MODEL, RKV_COLS, LORA_PAD, ATTN_WIDTH, KV_WIDTH)],
        compiler_params=pltpu.CompilerParams(dimension_semantics=("arbitrary",),
                                             vmem_limit_bytes=VMEM_LIMIT),
        name="ffn_in",
    )(x, g1, wg, wu, wd, g2, win, b_in)


def _dot(a, b):
    return jnp.dot(a, b, preferred_element_type=F32, precision=HI)


def _dot_nt(a, b):
    return lax.dot_general(a, b, (((1,), (1,)), ((), ())), preferred_element_type=F32, precision=HI)


def _dot_tn(a, b):
    return lax.dot_general(a, b, (((0,), (0,)), ((), ())), preferred_element_type=F32, precision=HI)


def _unit_lower_inverse(a, row, col):
    eye = (row == col).astype(F32)
    same = lambda n: (row // n) == (col // n)
    a8 = jnp.where(same(8), a, 0.0)
    a2 = _dot(a8, a8)
    a4 = _dot(a2, a2)
    inv = eye + a8
    inv = inv + _dot(inv, a2)
    inv = inv + _dot(inv, a4)
    for n in (16, 32, 64):
        c = jnp.where(same(n) & jnp.logical_not(same(n // 2)), a, 0.0)
        inv = inv + _dot(inv, _dot(c, inv))
    return inv


def _shift_lerp(z, prev, mu, row0):
    shifted = jnp.where(row0, prev, pltpu.roll(z, 1, axis=0))
    return z + (shifted - z) * mu


def _rwkv_kernel(zrkv_ref, zlora_ref, s0_ref, prev_rkv_ref, prev_lora_ref, mu_rkv_ref, mu_lora_ref,
                 wlora_ref, w0_ref, a0_ref, kk_ref, ka_ref, rk_ref, lnw_ref, lnb_ref,
                 y_ref, sfin_ref, s_scr, prkv_scr, plora_scr):
    c = pl.program_id(1)

    @pl.when(c == 0)
    def _():
        s_scr[...] = s0_ref[...]
        prkv_scr[...] = prev_rkv_ref[...]
        plora_scr[...] = prev_lora_ref[...]

    T = CHUNK
    zrkv_raw = zrkv_ref[0]
    zlora_raw = zlora_ref[0]
    row0 = lax.broadcasted_iota(jnp.int32, (T, 1), 0) == 0
    zrkv = _shift_lerp(zrkv_raw, prkv_scr[...], mu_rkv_ref[...], row0)
    zlora = _shift_lerp(zlora_raw, plora_scr[...], mu_lora_ref[...], row0)
    prkv_scr[...] = zrkv_raw[T - 1:T, :]
    plora_scr[...] = zlora_raw[T - 1:T, :]

    zr = zrkv[:, 0:RWKV_WIDTH]
    zk = zrkv[:, RWKV_WIDTH:2 * RWKV_WIDTH]
    zv = zrkv[:, 2 * RWKV_WIDTH:3 * RWKV_WIDTH]

    lane = lax.broadcasted_iota(jnp.int32, (T, LORA_PAD), 1)
    act = jnp.where(lane < LORA_W, jnp.tanh(zlora),
                    jnp.where(lane < LORA_W + LORA_A, zlora, jax.nn.sigmoid(zlora)))
    lo = _dot(act, wlora_ref[...])
    lw = -jnp.exp(F32(-0.5)) * jax.nn.sigmoid(w0_ref[...] + lo[:, 0:RWKV_WIDTH])
    a = jax.nn.sigmoid(a0_ref[...] + lo[:, RWKV_WIDTH:2 * RWKV_WIDTH])
    g = lo[:, 2 * RWKV_WIDTH:3 * RWKV_WIDTH]

    row = lax.broadcasted_iota(jnp.int32, (T, T), 0)
    col = lax.broadcasted_iota(jnp.int32, (T, T), 1)
    tri = (col <= row).astype(F32)
    cum = _dot(tri, lw)
    cen = cum - cum[T // 2 - 1:T // 2, :]
    e_pos = jnp.exp(cen)
    e_neg = jnp.exp(-cen)
    e_prev = jnp.exp(cen - lw)
    e_end = jnp.exp(cum[T - 1:T, :] - cum)
    w_tot = jnp.exp(cum[T - 1:T, :])

    kkn = zk * kk_ref[...]
    k = zk * (1.0 + (a - 1.0) * ka_ref[...])
    rk = zr * k * rk_ref[...]
    lnw = lnw_ref[...]
    lnb = lnb_ref[...]

    strict = col < row
    incl = col <= row
    outs = []
    for h in range(RWKV_HEADS):
        sl = slice(h * HEAD, (h + 1) * HEAD)
        kk_h = kkn[:, sl]
        nrm = jnp.sqrt(jnp.sum(kk_h * kk_h, axis=-1, keepdims=True))
        kk_h = kk_h / jnp.maximum(nrm, 1e-12)
        b_h = kk_h * a[:, sl]
        v_h = zv[:, sl]
        k_h = k[:, sl]
        at = -kk_h * e_prev[:, sl]
        rt = zr[:, sl] * e_pos[:, sl]
        bt = b_h * e_neg[:, sl]
        kt = k_h * e_neg[:, sl]
        s_h = s_scr[h]

        a_ab = jnp.where(strict, _dot_nt(at, bt), 0.0)
        a_ak = jnp.where(strict, _dot_nt(at, kt), 0.0)
        a_rb = jnp.where(incl, _dot_nt(rt, bt), 0.0)
        a_rk = jnp.where(incl, _dot_nt(rt, kt), 0.0)
        inv = _unit_lower_inverse(a_ab, row, col)
        e0 = jnp.exp(cum[T // 2 - 1:T // 2, sl])
        u = _dot(inv, _dot_nt(at * e0, s_h) + _dot(a_ak, v_h))
        y = _dot_nt(rt * e0, s_h) + _dot(a_rb, u) + _dot(a_rk, v_h)
        s_scr[h] = (s_h * w_tot[:, sl]
                    + _dot_tn(u, b_h * e_end[:, sl]) + _dot_tn(v_h, k_h * e_end[:, sl]))

        mean = jnp.mean(y, axis=-1, keepdims=True)
        yc = y - mean
        var = jnp.mean(yc * yc, axis=-1, keepdims=True)
        yn = yc * lax.rsqrt(var + GN_EPS) * lnw[:, sl] + lnb[:, sl]
        bonus = jnp.sum(rk[:, sl], axis=-1, keepdims=True) * v_h
        outs.append((yn + bonus) * g[:, sl])
    y_ref[0] = jnp.concatenate(outs, axis=-1)

    @pl.when(c == pl.num_programs(1) - 1)
    def _():
        sfin_ref[0] = s_scr[...]


def _rwkv(zrkv, zlora, s0, prev_rkv, prev_lora, p):
    B, T, _ = zrkv.shape
    blk = lambda w: pl.BlockSpec((1, CHUNK, w), lambda b, c: (b, c, 0))
    vec = lambda w: _const_spec((1, w))
    state = (RWKV_HEADS, HEAD, HEAD)
    return pl.pallas_call(
        _rwkv_kernel,
        grid=(B, T // CHUNK),
        in_specs=[blk(RKV_COLS), blk(LORA_PAD), _const_spec(state), vec(RKV_COLS), vec(LORA_PAD),
                  vec(RKV_COLS), vec(LORA_PAD), _const_spec((LORA_PAD, 3 * RWKV_WIDTH))]
                 + [vec(RWKV_WIDTH)] * 7,
        out_specs=[blk(RWKV_WIDTH), pl.BlockSpec((1,) + state, lambda b, c: (b, 0, 0, 0))],
        out_shape=[jax.ShapeDtypeStruct((B, T, RWKV_WIDTH), F32),
                   jax.ShapeDtypeStruct((B,) + state, F32)],
        scratch_shapes=[pltpu.VMEM(state, F32), pltpu.VMEM((1, RKV_COLS), F32),
                        pltpu.VMEM((1, LORA_PAD), F32)],
        compiler_params=pltpu.CompilerParams(dimension_semantics=("arbitrary", "arbitrary"),
                                             vmem_limit_bytes=VMEM_LIMIT),
        name="rwkv",
    )(zrkv, zlora, s0, prev_rkv, prev_lora, p["mu_rkv"], p["mu_lora"], p["wlora"], p["w0"], p["a0"],
      p["k_k"], p["k_a"], p["r_k"], p["ln_w"], p["ln_b"])


N_WIN = (WIN_CHUNKS + 1) * CHUNK
N_KEYS = N_WIN + N_META


def _attn_kernel(sink_ref, q_ref, kv2_ref, kv1_ref, kv0_ref, kvm_ref, o_ref):
    c = pl.program_id(1)
    q = q_ref[0]
    kv = jnp.concatenate([kv2_ref[0], kv1_ref[0], kv0_ref[0], kvm_ref[...]], axis=0)
    rows = ATTN_GROUP * CHUNK
    ri = lax.broadcasted_iota(jnp.int32, (rows, N_KEYS), 0)
    ci = lax.broadcasted_iota(jnp.int32, (rows, N_KEYS), 1)
    qi = ri % CHUNK
    gi = ri // CHUNK
    is_meta = ci >= N_WIN
    delta = jnp.where(is_meta, qi + c * CHUNK + N_META - (ci - N_WIN), qi + WIN_CHUNKS * CHUNK - ci)
    dist = jnp.abs(delta).astype(F32)
    valid = is_meta | (ci >= (WIN_CHUNKS - c) * CHUNK)
    g_col = lax.broadcasted_iota(jnp.int32, (rows, 1), 0) // CHUNK
    scale = HEAD ** -0.5
    outs = []
    for j in range(ATTN_KV_HEADS):
        k_j = kv[:, j * HEAD:(j + 1) * HEAD]
        v_j = kv[:, (ATTN_KV_HEADS + j) * HEAD:(ATTN_KV_HEADS + j + 1) * HEAD]
        q_j = jnp.concatenate([q[:, (ATTN_GROUP * j + g) * HEAD:(ATTN_GROUP * j + g + 1) * HEAD]
                               for g in range(ATTN_GROUP)], axis=0)
        slope = jnp.zeros((rows, N_KEYS), F32)
        sink = jnp.zeros((rows, 1), F32)
        for g in range(ATTN_GROUP):
            hd = ATTN_GROUP * j + g
            slope = jnp.where(gi == g, F32(2.0 ** (-8.0 * (hd + 1) / ATTN_HEADS)), slope)
            sink = jnp.where(g_col == g, sink_ref[hd], sink)
        s = _dot_nt(q_j, k_j) * scale - slope * dist
        s = jnp.where(valid, s, NEG_INF)
        m = jnp.maximum(jnp.max(s, axis=-1, keepdims=True), sink)
        p = jnp.exp(s - m)
        den = jnp.sum(p, axis=-1, keepdims=True) + jnp.exp(sink - m)
        o = _dot(p, v_j) / den
        outs.extend(o[g * CHUNK:(g + 1) * CHUNK] for g in range(ATTN_GROUP))
    o_ref[0] = jnp.concatenate(outs, axis=-1)


def _attn(q, kv, kv_meta, sinks):
    B, T, _ = q.shape
    back = lambda n: pl.BlockSpec((1, CHUNK, KV_WIDTH), lambda b, c: (b, jnp.maximum(c - n, 0), 0))
    return pl.pallas_call(
        _attn_kernel,
        grid=(B, T // CHUNK),
        in_specs=[pl.BlockSpec(memory_space=pltpu.SMEM),
                  pl.BlockSpec((1, CHUNK, ATTN_WIDTH), lambda b, c: (b, c, 0)),
                  back(2), back(1), back(0), _const_spec((N_META, KV_WIDTH))],
        out_specs=pl.BlockSpec((1, CHUNK, ATTN_WIDTH), lambda b, c: (b, c, 0)),
        out_shape=jax.ShapeDtypeStruct((B, T, ATTN_WIDTH), F32),
        compiler_params=pltpu.CompilerParams(dimension_semantics=("arbitrary", "arbitrary"),
                                             vmem_limit_bytes=VMEM_LIMIT),
        name="attn",
    )(sinks, q, kv, kv, kv, kv_meta)


def _out_ffn_kernel(h_ref, yr_ref, ya_ref, wo_ref, g_ref, wg_ref, wu_ref, wd_ref, gf_ref, o_ref):
    y = jnp.concatenate([yr_ref[...], ya_ref[...]], axis=-1).astype(BF16)
    h = h_ref[...] + jnp.dot(y, wo_ref[...], preferred_element_type=F32)
    n = _rmsnorm(h, g_ref[...]).astype(BF16)
    h = h + 0.5 * _swiglu(n, wg_ref, wu_ref, wd_ref)
    o_ref[...] = _rmsnorm(h, gf_ref[...])


def _out_ffn(h, y_rwkv, y_attn, wo, g, wg, wu, wd, gf, tm):
    rows = h.shape[0]
    row = lambda w: pl.BlockSpec((tm, w), lambda i: (i, 0))
    return pl.pallas_call(
        _out_ffn_kernel,
        grid=(rows // tm,),
        in_specs=[row(D_MODEL), row(RWKV_WIDTH), row(ATTN_WIDTH), _const_spec((D_MODEL, D_MODEL)),
                  _const_spec((1, D_MODEL)), _const_spec((D_MODEL, D_FF)), _const_spec((D_MODEL, D_FF)),
                  _const_spec((D_FF, D_MODEL)), _const_spec((1, D_MODEL))],
        out_specs=row(D_MODEL),
        out_shape=jax.ShapeDtypeStruct((rows, D_MODEL), F32),
        compiler_params=pltpu.CompilerParams(dimension_semantics=("arbitrary",),
                                             vmem_limit_bytes=VMEM_LIMIT),
        name="out_ffn",
    )(h, y_rwkv, y_attn, wo, g, wg, wu, wd, gf)


def _pack_w_in(w_in, b_attn):
    rwkv_cols = RKV_COLS + LORA_COLS
    pad = jnp.zeros((D_MODEL, LORA_PAD - LORA_COLS), w_in.dtype)
    w = jnp.concatenate([w_in[:, :rwkv_cols], pad, w_in[:, rwkv_cols:]], axis=1).astype(BF16)
    b = jnp.concatenate([jnp.zeros((RKV_COLS + LORA_PAD,), F32), b_attn.astype(F32)])[None]
    return w, b


def _pack_lora(w2, a2, g2):
    w = jnp.zeros((LORA_PAD, 3 * RWKV_WIDTH), F32)
    w = w.at[0:LORA_W, 0:RWKV_WIDTH].set(w2)
    w = w.at[LORA_W:LORA_W + LORA_A, RWKV_WIDTH:2 * RWKV_WIDTH].set(a2)
    w = w.at[LORA_W + LORA_A:LORA_COLS, 2 * RWKV_WIDTH:].set(g2)
    return w


def kernel(x, meta_tokens, ffn1_norm, ffn1_w_gate, ffn1_w_up, ffn1_w_down, mix_norm, w_in, b_attn, rwkv_mu, rwkv_w0, rwkv_w2, rwkv_a0, rwkv_a2, rwkv_g2, rwkv_k_k, rwkv_k_a, rwkv_r_k, rwkv_ln_w, rwkv_ln_b, attn_sinks, w_out, ffn2_norm, ffn2_w_gate, ffn2_w_up, ffn2_w_down, final_norm):
    assert ffn1_norm.shape[0] == 1, "single-layer trunk"
    B, T, D = x.shape
    row = lambda v: v.reshape(1, -1).astype(F32)
    bf = lambda w: w[0].astype(BF16)

    win, b_in = _pack_w_in(w_in[0], b_attn[0])
    mu = rwkv_mu[0].astype(F32)
    mu_lora = jnp.concatenate([mu[RKV_COLS:], jnp.zeros((LORA_PAD - LORA_COLS,), F32)])
    rp = dict(mu_rkv=row(mu[:RKV_COLS]), mu_lora=row(mu_lora),
              wlora=_pack_lora(rwkv_w2[0], rwkv_a2[0], rwkv_g2[0]),
              w0=row(rwkv_w0[0]), a0=row(rwkv_a0[0]), k_k=row(rwkv_k_k[0]), k_a=row(rwkv_k_a[0]),
              r_k=row(rwkv_r_k[0]), ln_w=row(rwkv_ln_w[0]), ln_b=row(rwkv_ln_b[0]))
    ffn1 = (row(ffn1_norm[0]), bf(ffn1_w_gate), bf(ffn1_w_up), bf(ffn1_w_down))

    xm = jnp.concatenate([jnp.zeros((CHUNK - N_META, D), F32), meta_tokens.astype(F32)], axis=0)
    _, zrkv_m, zlora_m, _, kv_m = _ffn_in(xm, *ffn1, row(mix_norm[0]), win, b_in, tm=CHUNK)
    zeros_state = jnp.zeros((RWKV_HEADS, HEAD, HEAD), F32)
    _, s_meta = _rwkv(zrkv_m[None], zlora_m[None], zeros_state,
                      jnp.zeros((1, RKV_COLS), F32), jnp.zeros((1, LORA_PAD), F32), rp)

    xf = x.reshape(B * T, D)
    h1, zrkv, zlora, q, kv = _ffn_in(xf, *ffn1, row(mix_norm[0]), win, b_in, tm=256)
    y_rwkv, _ = _rwkv(zrkv.reshape(B, T, -1), zlora.reshape(B, T, -1), s_meta[0],
                      zrkv_m[CHUNK - 1:], zlora_m[CHUNK - 1:], rp)
    y_attn = _attn(q.reshape(B, T, -1), kv.reshape(B, T, -1), kv_m[CHUNK - N_META:],
                   attn_sinks[0].astype(F32))
    out = _out_ffn(h1, y_rwkv.reshape(B * T, -1), y_attn.reshape(B * T, -1), bf(w_out),
                   row(ffn2_norm[0]), bf(ffn2_w_gate), bf(ffn2_w_up), bf(ffn2_w_down),
                   row(final_norm), tm=256)
    return out.reshape(B, T, D)
```

```python
import functools

import jax
import jax.numpy as jnp
from jax import lax
from jax.experimental import pallas as pl
from jax.experimental.pallas import tpu as pltpu

F32 = jnp.float32
BF16 = jnp.bfloat16

D_MODEL = 1024
D_FF = 2816
N_META = 16
NORM_EPS = 1e-5
CHUNK = 64

HEAD = 64
RWKV_HEADS = 8
RWKV_WIDTH = 512
LORA_W, LORA_A, LORA_G = 32, 32, 96
LORA_COLS = LORA_W + LORA_A + LORA_G
LORA_PAD = 256
GN_EPS = 64e-5

ATTN_HEADS = 8
ATTN_KV_HEADS = 2
ATTN_GROUP = 4
ATTN_WIDTH = 512
KV_WIDTH = 2 * ATTN_KV_HEADS * HEAD
WIN_CHUNKS = 2
NEG_INF = -1e30

RKV_COLS = 3 * RWKV_WIDTH
Z_COLS = RKV_COLS + LORA_PAD + ATTN_WIDTH + KV_WIDTH

RWKV_CHUNKS_PER_STEP = 4
FF_CHUNK = 256
VMEM_LIMIT = 56 * 1024 * 1024


def _const_spec(shape):
    nd = len(shape)
    return pl.BlockSpec(shape, lambda *_: (0,) * nd, pipeline_mode=pl.Buffered(1))


def _rmsnorm(x, g):
    return x * lax.rsqrt(jnp.mean(x * x, axis=-1, keepdims=True) + NORM_EPS) * g


def _swiglu(n, wg_ref, wu_ref, wd_ref):
    acc = jnp.zeros((n.shape[0], D_MODEL), F32)
    for c in range(D_FF // FF_CHUNK):
        sl = slice(c * FF_CHUNK, (c + 1) * FF_CHUNK)
        g = jnp.dot(n, wg_ref[:, sl], preferred_element_type=F32)
        u = jnp.dot(n, wu_ref[:, sl], preferred_element_type=F32)
        a = (g * jax.nn.sigmoid(g) * u).astype(BF16)
        acc = acc + jnp.dot(a, wd_ref[sl, :], preferred_element_type=F32)
    return acc


def _ffn_in_kernel(x_ref, g1_ref, wg_ref, wu_ref, wd_ref, g2_ref, win_ref, bin_ref,
                   h_ref, zrkv_ref, zlora_ref, q_ref, kv_ref):
    x = x_ref[...]
    n1 = _rmsnorm(x, g1_ref[...]).astype(BF16)
    h = x + 0.5 * _swiglu(n1, wg_ref, wu_ref, wd_ref)
    h_ref[...] = h
    n2 = _rmsnorm(h, g2_ref[...]).astype(BF16)
    z = jnp.dot(n2, win_ref[...], preferred_element_type=F32) + bin_ref[...]
    o = 0
    for ref, w in ((zrkv_ref, RKV_COLS), (zlora_ref, LORA_PAD), (q_ref, ATTN_WIDTH), (kv_ref, KV_WIDTH)):
        ref[...] = z[:, o:o + w]
        o += w


def _ffn_in(x, g1, wg, wu, wd, g2, win, b_in, tm):
    rows = x.shape[0]
    row = lambda w: pl.BlockSpec((tm, w), lambda i: (i, 0))
    return pl.pallas_call(
        _ffn_in_kernel,
        grid=(rows // tm,),
        in_specs=[row(D_MODEL), _const_spec((1, D_MODEL)), _const_spec((D_MODEL, D_FF)),
                  _const_spec((D_MODEL, D_FF)), _const_spec((D_FF, D_MODEL)), _const_spec((1, D_MODEL)),
                  _const_spec((D_MODEL, Z_COLS)), _const_spec((1, Z_COLS))],
        out_specs=[row(D_MODEL), row(RKV_COLS), row(LORA_PAD), row(ATTN_WIDTH), row(KV_WIDTH)],
        out_shape=[jax.ShapeDtypeStruct((rows, w), F32)
                   for w in (D_MODEL, RKV_COLS, LORA_PAD, ATTN_WIDTH, KV_WIDTH)],
        compiler_params=pltpu.CompilerParams(dimension_semantics=("arbitrary",),
                                             vmem_limit_bytes=VMEM_LIMIT),
        name="ffn_in",
    )(x, g1, wg, wu, wd, g2, win, b_in)


QUAD = 4 * HEAD
N_QUADS = RWKV_WIDTH // QUAD
INV_PASSES = 1
MIX_PASSES = 1
STATE_PASSES = 1


def _blockdiag(x, bd_ref):
    return jnp.concatenate([x * bd_ref[h * HEAD:(h + 1) * HEAD, :] for h in range(4)], axis=0)


def _mm(lhs, rhs, bd_ref, passes=1, nt=False):
    lh = lhs.astype(BF16)
    rh = rhs.astype(BF16)
    if passes == 1:
        l_cat, r_cat = lh, _blockdiag(rh, bd_ref)
    else:
        ll = (lhs - lh.astype(F32)).astype(BF16)
        rl = (rhs - rh.astype(F32)).astype(BF16)
        rh_bd, rl_bd = _blockdiag(rh, bd_ref), _blockdiag(rl, bd_ref)
        l_cat = jnp.concatenate([lh, ll, lh], axis=1)
        r_cat = jnp.concatenate([rh_bd, rh_bd, rl_bd], axis=1 if nt else 0)
    dims = (((1,), (1,)), ((), ())) if nt else (((1,), (0,)), ((), ()))
    return lax.dot_general(l_cat, r_cat, dims, preferred_element_type=F32)


def _mm_tn_diag(lhs_list, rhs_list, head_of_lane):
    lt = jnp.concatenate([l.T for l in lhs_list], axis=1).astype(BF16)
    r = jnp.concatenate(rhs_list, axis=0).astype(BF16)
    full = jnp.dot(lt, r, preferred_element_type=F32)
    out = full[3 * HEAD:4 * HEAD]
    for h in (2, 1, 0):
        out = jnp.where(head_of_lane == h, full[h * HEAD:(h + 1) * HEAD], out)
    return out


def _unit_lower_inverse(a_list, t_idx, j_idx, bd_ref):
    mm = functools.partial(_mm, bd_ref=bd_ref, passes=INV_PASSES)
    same = lambda n: (t_idx // n) == (j_idx // n)
    eye = (t_idx == j_idx).astype(F32)
    a8 = [jnp.where(same(8), a, 0.0) for a in a_list]
    a2 = [mm(x, x) for x in a8]
    a4 = [mm(x, x) for x in a2]
    inv = [eye + x for x in a8]
    inv = [i + mm(i, x) for i, x in zip(inv, a2)]
    inv = [i + mm(i, x) for i, x in zip(inv, a4)]
    for n in (16, 32, 64):
        off = same(n) & jnp.logical_not(same(n // 2))
        ci = [mm(jnp.where(off, a, 0.0), i) for a, i in zip(a_list, inv)]
        inv = [i + mm(i, x) for i, x in zip(inv, ci)]
    return inv


def _shift_lerp(z, prev, mu, row0):
    shifted = jnp.where(row0, prev, pltpu.roll(z, 1, axis=0))
    return z + (shifted - z) * mu


def _group_sum(x, bd_ref):
    return jnp.concatenate(
        [jnp.dot(x[:, q * QUAD:(q + 1) * QUAD].astype(BF16), bd_ref[...], preferred_element_type=F32)
         for q in range(N_QUADS)], axis=1)


def _rwkv_kernel(zrkv_ref, zlora_ref, s0_ref, prev_rkv_ref, prev_lora_ref, mu_rkv_ref, mu_lora_ref,
                 wlora_ref, w0_ref, a0_ref, kk_ref, ka_ref, rk_ref, lnw_ref, lnb_ref, bd_ref,
                 y_ref, sfin_ref, s_scr, prkv_scr, plora_scr, *, nc):
    step = pl.program_id(1)

    @pl.when(step == 0)
    def _():
        s_scr[...] = s0_ref[...]
        prkv_scr[...] = prev_rkv_ref[...]
        plora_scr[...] = prev_lora_ref[...]

    T = CHUNK
    R = nc * T
    zrkv_raw = zrkv_ref[0]
    zlora_raw = zlora_ref[0]
    row0 = lax.broadcasted_iota(jnp.int32, (R, 1), 0) == 0
    zrkv = _shift_lerp(zrkv_raw, prkv_scr[...], mu_rkv_ref[...], row0)
    zlora = _shift_lerp(zlora_raw, plora_scr[...], mu_lora_ref[...], row0)
    prkv_scr[...] = zrkv_raw[R - 1:R, :]
    plora_scr[...] = zlora_raw[R - 1:R, :]

    zr = zrkv[:, 0:RWKV_WIDTH]
    zk = zrkv[:, RWKV_WIDTH:2 * RWKV_WIDTH]
    zv = zrkv[:, 2 * RWKV_WIDTH:3 * RWKV_WIDTH]

    lane = lax.broadcasted_iota(jnp.int32, (R, LORA_PAD), 1)
    act = jnp.where(lane < LORA_W, jnp.tanh(zlora),
                    jnp.where(lane < LORA_W + LORA_A, zlora, jax.nn.sigmoid(zlora)))
    lo = jnp.dot(act.astype(BF16), wlora_ref[...], preferred_element_type=F32)
    lw = -jnp.exp(F32(-0.5)) * jax.nn.sigmoid(w0_ref[...] + lo[:, 0:RWKV_WIDTH])
    a = jax.nn.sigmoid(a0_ref[...] + lo[:, RWKV_WIDTH:2 * RWKV_WIDTH])
    g = lo[:, 2 * RWKV_WIDTH:3 * RWKV_WIDTH]

    kkn = zk * kk_ref[...]
    kk = kkn * jnp.minimum(lax.rsqrt(_group_sum(kkn * kkn, bd_ref)), 1e12)
    k = zk * (1.0 + (a - 1.0) * ka_ref[...])
    b = kk * a
    bonus = _group_sum(zr * k * rk_ref[...], bd_ref) * zv

    t_idx = lax.broadcasted_iota(jnp.int32, (T, QUAD), 0)
    l_idx = lax.broadcasted_iota(jnp.int32, (T, QUAD), 1)
    j_idx = l_idx % HEAD
    head_of_lane = l_idx // HEAD
    strict = j_idx < t_idx
    incl = j_idx <= t_idx
    mm = functools.partial(_mm, bd_ref=bd_ref, passes=MIX_PASSES)

    rr = lax.broadcasted_iota(jnp.int32, (R, R), 0)
    cc = lax.broadcasted_iota(jnp.int32, (R, R), 1)
    same_chunk = (rr // T) == (cc // T)
    ones = lambda m: jnp.where(m, 1.0, 0.0).astype(BF16)
    sel = jnp.concatenate([ones(same_chunk & (cc <= rr)), ones(same_chunk & (cc % T < T // 2)),
                           ones(same_chunk)], axis=0)
    lw_hi = lw.astype(BF16)
    lw_lo = (lw - lw_hi.astype(F32)).astype(BF16)
    sums = (jnp.dot(sel, lw_hi, preferred_element_type=F32)
            + jnp.dot(sel, lw_lo, preferred_element_type=F32))
    cum, mid, end = sums[0:R], sums[R:2 * R], sums[2 * R:3 * R]
    e = jnp.exp(cum)
    first = lax.broadcasted_iota(jnp.int32, (R, 1), 0) % T == 0
    e_prev = jnp.where(first, 1.0, pltpu.roll(e, 1, axis=0))
    e_mid = jnp.exp(-mid)
    e_neg = jnp.exp(mid - cum)
    e_end = jnp.exp(end - cum)
    at0 = -kk * e_prev
    rt0 = zr * e
    at = at0 * e_mid
    rt = rt0 * e_mid
    bt = b * e_neg
    kt = k * e_neg
    b_end = b * e_end
    k_end = k * e_end

    probs = [(n, q) for n in range(nc) for q in range(N_QUADS)]
    cut = lambda arr, p: arr[p[0] * T:(p[0] + 1) * T, p[1] * QUAD:(p[1] + 1) * QUAD]
    xs = [jnp.concatenate([cut(at, p), cut(rt, p)], axis=0) for p in probs]
    xb = [mm(x, cut(bt, p), nt=True) for x, p in zip(xs, probs)]
    xk = [mm(x, cut(kt, p), nt=True) for x, p in zip(xs, probs)]
    a_ab = [jnp.where(strict, v[:T], 0.0) for v in xb]
    a_rb = [jnp.where(incl, v[T:], 0.0) for v in xb]
    a_ak = [jnp.where(strict, v[:T], 0.0) for v in xk]
    a_rk = [jnp.where(incl, v[T:], 0.0) for v in xk]
    w1 = [mm(m, cut(zv, p)) for m, p in zip(a_ak, probs)]
    yv = [mm(m, cut(zv, p)) for m, p in zip(a_rk, probs)]
    inv = _unit_lower_inverse(a_ab, t_idx, j_idx, bd_ref)
    au = [mm(i, cut(at0, p)) for i, p in zip(inv, probs)]
    u0 = [mm(i, w) for i, w in zip(inv, w1)]
    r_eff = [cut(rt0, p) + mm(m, x) for m, x, p in zip(a_rb, au, probs)]
    y0 = [v + mm(m, u) for v, m, u in zip(yv, a_rb, u0)]
    m_all = [_mm_tn_diag([x], [cut(b_end, p)], head_of_lane) for x, p in zip(au, probs)]
    g_all = [_mm_tn_diag([u, cut(zv, p)], [cut(b_end, p), cut(k_end, p)], head_of_lane)
             for u, p in zip(u0, probs)]

    mm_s = functools.partial(_mm, bd_ref=bd_ref, passes=STATE_PASSES)
    s = [s_scr[:, q * QUAD:(q + 1) * QUAD] for q in range(N_QUADS)]
    y_rows = []
    for n in range(nc):
        ys = []
        for q in range(N_QUADS):
            i = n * N_QUADS + q
            w_tot = e[(n + 1) * T - 1:(n + 1) * T, q * QUAD:(q + 1) * QUAD]
            s_new = s[q] * w_tot + mm_s(s[q], m_all[i]) + g_all[i]
            ys.append(y0[i] + mm_s(r_eff[i], s[q], nt=True))
            s[q] = s_new
        y_rows.append(jnp.concatenate(ys, axis=1))
    for q in range(N_QUADS):
        s_scr[:, q * QUAD:(q + 1) * QUAD] = s[q]
    y = jnp.concatenate(y_rows, axis=0)
    yc = y - _group_sum(y, bd_ref) * (1.0 / HEAD)
    var = _group_sum(yc * yc, bd_ref) * (1.0 / HEAD)
    yn = yc * lax.rsqrt(var + GN_EPS) * lnw_ref[...] + lnb_ref[...]
    y_ref[0] = ((yn + bonus) * g).astype(y_ref.dtype)

    @pl.when(step == pl.num_programs(1) - 1)
    def _():
        sfin_ref[0] = s_scr[...]


def _rwkv(zrkv, zlora, s0, prev_rkv, prev_lora, p, nc):
    B, T, _ = zrkv.shape
    rows = nc * CHUNK
    blk = lambda w: pl.BlockSpec((1, rows, w), lambda b, c: (b, c, 0))
    vec = lambda w: _const_spec((1, w))
    state = (HEAD, RWKV_WIDTH)
    return pl.pallas_call(
        functools.partial(_rwkv_kernel, nc=nc),
        grid=(B, T // rows),
        in_specs=[blk(RKV_COLS), blk(LORA_PAD), _const_spec(state), vec(RKV_COLS), vec(LORA_PAD),
                  vec(RKV_COLS), vec(LORA_PAD), _const_spec((LORA_PAD, 3 * RWKV_WIDTH))]
                 + [vec(RWKV_WIDTH)] * 7 + [_const_spec((QUAD, QUAD))],
        out_specs=[blk(RWKV_WIDTH), pl.BlockSpec((1,) + state, lambda b, c: (b, 0, 0))],
        out_shape=[jax.ShapeDtypeStruct((B, T, RWKV_WIDTH), BF16),
                   jax.ShapeDtypeStruct((B,) + state, F32)],
        scratch_shapes=[pltpu.VMEM(state, F32), pltpu.VMEM((1, RKV_COLS), F32),
                        pltpu.VMEM((1, LORA_PAD), F32)],
        compiler_params=pltpu.CompilerParams(dimension_semantics=("arbitrary", "arbitrary"),
                                             vmem_limit_bytes=VMEM_LIMIT),
        name="rwkv",
    )(zrkv, zlora, s0, prev_rkv, prev_lora, p["mu_rkv"], p["mu_lora"], p["wlora"], p["w0"], p["a0"],
      p["k_k"], p["k_a"], p["r_k"], p["ln_w"], p["ln_b"], p["bd"])


N_WIN = (WIN_CHUNKS + 1) * CHUNK
N_KEYS = N_WIN + N_META


def _attn_kernel(sink_ref, q_ref, kv2_ref, kv1_ref, kv0_ref, kvm_ref, o_ref):
    c = pl.program_id(1)
    q = q_ref[0]
    kv = jnp.concatenate([kv2_ref[0], kv1_ref[0], kv0_ref[0], kvm_ref[...]], axis=0)
    rows = ATTN_GROUP * CHUNK
    ri = lax.broadcasted_iota(jnp.int32, (rows, N_KEYS), 0)
    ci = lax.broadcasted_iota(jnp.int32, (rows, N_KEYS), 1)
    qi = ri % CHUNK
    gi = ri // CHUNK
    is_meta = ci >= N_WIN
    delta = jnp.where(is_meta, qi + c * CHUNK + N_META - (ci - N_WIN), qi + WIN_CHUNKS * CHUNK - ci)
    dist = jnp.abs(delta).astype(F32)
    valid = is_meta | (ci >= (WIN_CHUNKS - c) * CHUNK)
    g_col = lax.broadcasted_iota(jnp.int32, (rows, 1), 0) // CHUNK
    scale = HEAD ** -0.5
    outs = []
    for j in range(ATTN_KV_HEADS):
        k_j = kv[:, j * HEAD:(j + 1) * HEAD]
        v_j = kv[:, (ATTN_KV_HEADS + j) * HEAD:(ATTN_KV_HEADS + j + 1) * HEAD]
        q_j = jnp.concatenate([q[:, (ATTN_GROUP * j + g) * HEAD:(ATTN_GROUP * j + g + 1) * HEAD]
                               for g in range(ATTN_GROUP)], axis=0)
        slope = jnp.zeros((rows, N_KEYS), F32)
        sink = jnp.zeros((rows, 1), F32)
        for g in range(ATTN_GROUP):
            hd = ATTN_GROUP * j + g
            slope = jnp.where(gi == g, F32(2.0 ** (-8.0 * (hd + 1) / ATTN_HEADS)), slope)
            sink = jnp.where(g_col == g, sink_ref[hd], sink)
        s = lax.dot_general(q_j.astype(BF16), k_j.astype(BF16), (((1,), (1,)), ((), ())),
                            preferred_element_type=F32) * scale - slope * dist
        s = jnp.where(valid, s, NEG_INF)
        m = jnp.maximum(jnp.max(s, axis=-1, keepdims=True), sink)
        p = jnp.exp(s - m)
        den = jnp.sum(p, axis=-1, keepdims=True) + jnp.exp(sink - m)
        o = jnp.dot(p.astype(BF16), v_j.astype(BF16), preferred_element_type=F32) / den
        outs.extend(o[g * CHUNK:(g + 1) * CHUNK] for g in range(ATTN_GROUP))
    o_ref[0] = jnp.concatenate(outs, axis=-1)


def _attn(q, kv, kv_meta, sinks):
    B, T, _ = q.shape
    back = lambda n: pl.BlockSpec((1, CHUNK, KV_WIDTH), lambda b, c: (b, jnp.maximum(c - n, 0), 0))
    return pl.pallas_call(
        _attn_kernel,
        grid=(B, T // CHUNK),
        in_specs=[pl.BlockSpec(memory_space=pltpu.SMEM),
                  pl.BlockSpec((1, CHUNK, ATTN_WIDTH), lambda b, c: (b, c, 0)),
                  back(2), back(1), back(0), _const_spec((N_META, KV_WIDTH))],
        out_specs=pl.BlockSpec((1, CHUNK, ATTN_WIDTH), lambda b, c: (b, c, 0)),
        out_shape=jax.ShapeDtypeStruct((B, T, ATTN_WIDTH), F32),
        compiler_params=pltpu.CompilerParams(dimension_semantics=("arbitrary", "arbitrary"),
                                             vmem_limit_bytes=VMEM_LIMIT),
        name="attn",
    )(sinks, q, kv, kv, kv, kv_meta)


def _out_ffn_kernel(h_ref, yr_ref, ya_ref, wo_ref, g_ref, wg_ref, wu_ref, wd_ref, gf_ref, o_ref):
    y = jnp.concatenate([yr_ref[...].astype(BF16), ya_ref[...].astype(BF16)], axis=-1)
    h = h_ref[...] + jnp.dot(y, wo_ref[...], preferred_element_type=F32)
    n = _rmsnorm(h, g_ref[...]).astype(BF16)
    h = h + 0.5 * _swiglu(n, wg_ref, wu_ref, wd_ref)
    o_ref[...] = _rmsnorm(h, gf_ref[...])


def _out_ffn(h, y_rwkv, y_attn, wo, g, wg, wu, wd, gf, tm):
    rows = h.shape[0]
    row = lambda w: pl.BlockSpec((tm, w), lambda i: (i, 0))
    return pl.pallas_call(
        _out_ffn_kernel,
        grid=(rows // tm,),
        in_specs=[row(D_MODEL), row(RWKV_WIDTH), row(ATTN_WIDTH), _const_spec((D_MODEL, D_MODEL)),
                  _const_spec((1, D_MODEL)), _const_spec((D_MODEL, D_FF)), _const_spec((D_MODEL, D_FF)),
                  _const_spec((D_FF, D_MODEL)), _const_spec((1, D_MODEL))],
        out_specs=row(D_MODEL),
        out_shape=jax.ShapeDtypeStruct((rows, D_MODEL), F32),
        compiler_params=pltpu.CompilerParams(dimension_semantics=("arbitrary",),
                                             vmem_limit_bytes=VMEM_LIMIT),
        name="out_ffn",
    )(h, y_rwkv, y_attn, wo, g, wg, wu, wd, gf)


def _pack_w_in(w_in, b_attn):
    rwkv_cols = RKV_COLS + LORA_COLS
    pad = jnp.zeros((D_MODEL, LORA_PAD - LORA_COLS), w_in.dtype)
    w = jnp.concatenate([w_in[:, :rwkv_cols], pad, w_in[:, rwkv_cols:]], axis=1).astype(BF16)
    b = jnp.concatenate([jnp.zeros((RKV_COLS + LORA_PAD,), F32), b_attn.astype(F32)])[None]
    return w, b


def _pack_lora(w2, a2, g2):
    w = jnp.zeros((LORA_PAD, 3 * RWKV_WIDTH), F32)
    w = w.at[0:LORA_W, 0:RWKV_WIDTH].set(w2)
    w = w.at[LORA_W:LORA_W + LORA_A, RWKV_WIDTH:2 * RWKV_WIDTH].set(a2)
    w = w.at[LORA_W + LORA_A:LORA_COLS, 2 * RWKV_WIDTH:].set(g2)
    return w


def _block_ones():
    i = jnp.arange(QUAD) // HEAD
    return (i[:, None] == i[None, :]).astype(BF16)


def kernel(x, meta_tokens, ffn1_norm, ffn1_w_gate, ffn1_w_up, ffn1_w_down, mix_norm, w_in, b_attn, rwkv_mu, rwkv_w0, rwkv_w2, rwkv_a0, rwkv_a2, rwkv_g2, rwkv_k_k, rwkv_k_a, rwkv_r_k, rwkv_ln_w, rwkv_ln_b, attn_sinks, w_out, ffn2_norm, ffn2_w_gate, ffn2_w_up, ffn2_w_down, final_norm):
    assert ffn1_norm.shape[0] == 1, "single-layer trunk"
    B, T, D = x.shape
    row = lambda v: v.reshape(1, -1).astype(F32)
    bf = lambda w: w[0].astype(BF16)

    win, b_in = _pack_w_in(w_in[0], b_attn[0])
    mu = rwkv_mu[0].astype(F32)
    mu_lora = jnp.concatenate([mu[RKV_COLS:], jnp.zeros((LORA_PAD - LORA_COLS,), F32)])
    rp = dict(mu_rkv=row(mu[:RKV_COLS]), mu_lora=row(mu_lora),
              wlora=_pack_lora(rwkv_w2[0], rwkv_a2[0], rwkv_g2[0]).astype(BF16), bd=_block_ones(),
              w0=row(rwkv_w0[0]), a0=row(rwkv_a0[0]), k_k=row(rwkv_k_k[0]), k_a=row(rwkv_k_a[0]),
              r_k=row(rwkv_r_k[0]), ln_w=row(rwkv_ln_w[0]), ln_b=row(rwkv_ln_b[0]))
    ffn1 = (row(ffn1_norm[0]), bf(ffn1_w_gate), bf(ffn1_w_up), bf(ffn1_w_down))

    xm = jnp.concatenate([jnp.zeros((CHUNK - N_META, D), F32), meta_tokens.astype(F32)], axis=0)
    _, zrkv_m, zlora_m, _, kv_m = _ffn_in(xm, *ffn1, row(mix_norm[0]), win, b_in, tm=CHUNK)
    zeros_state = jnp.zeros((HEAD, RWKV_WIDTH), F32)
    _, s_meta = _rwkv(zrkv_m[None], zlora_m[None], zeros_state,
                      jnp.zeros((1, RKV_COLS), F32), jnp.zeros((1, LORA_PAD), F32), rp, nc=1)

    xf = x.reshape(B * T, D)
    h1, zrkv, zlora, q, kv = _ffn_in(xf, *ffn1, row(mix_norm[0]), win, b_in, tm=256)
    y_rwkv, _ = _rwkv(zrkv.reshape(B, T, -1), zlora.reshape(B, T, -1), s_meta[0],
                      zrkv_m[CHUNK - 1:], zlora_m[CHUNK - 1:], rp, nc=RWKV_CHUNKS_PER_STEP)
    y_attn = _attn(q.reshape(B, T, -1), kv.reshape(B, T, -1), kv_m[CHUNK - N_META:],
                   attn_sinks[0].astype(F32))
    out = _out_ffn(h1, y_rwkv.reshape(B * T, -1), y_attn.reshape(B * T, -1), bf(w_out),
                   row(ffn2_norm[0]), bf(ffn2_w_gate), bf(ffn2_w_up), bf(ffn2_w_down),
                   row(final_norm), tm=256)
    return out.reshape(B, T, D)
```

```python
import functools

import jax
import jax.numpy as jnp
import numpy as np
from jax import lax
from jax.experimental import pallas as pl
from jax.experimental.pallas import tpu as pltpu

F32 = jnp.float32
BF16 = jnp.bfloat16

D_MODEL = 1024
D_FF = 2816
N_META = 16
NORM_EPS = 1e-5
CHUNK = 64

HEAD = 64
RWKV_HEADS = 8
RWKV_WIDTH = 512
LORA_W, LORA_A, LORA_G = 32, 32, 96
LORA_COLS = LORA_W + LORA_A + LORA_G
LORA_PAD = 256
GN_EPS = 64e-5

ATTN_HEADS = 8
ATTN_KV_HEADS = 2
ATTN_GROUP = 4
ATTN_WIDTH = 512
KV_WIDTH = 2 * ATTN_KV_HEADS * HEAD
WIN_CHUNKS = 2
NEG_INF = -1e30

RKV_COLS = 3 * RWKV_WIDTH
Z_COLS = RKV_COLS + LORA_PAD + ATTN_WIDTH + KV_WIDTH

RWKV_CHUNKS_PER_STEP = 4
FFN_ROWS = 512
FF_CHUNK = 256
VMEM_LIMIT = 56 * 1024 * 1024


def _const_spec(shape):
    nd = len(shape)
    return pl.BlockSpec(shape, lambda *_: (0,) * nd, pipeline_mode=pl.Buffered(1))


def _rmsnorm(x, g):
    return x * lax.rsqrt(jnp.mean(x * x, axis=-1, keepdims=True) + NORM_EPS) * g


def _swiglu(n, wg_ref, wu_ref, wd_ref):
    acc = jnp.zeros((n.shape[0], D_MODEL), F32)
    for c in range(D_FF // FF_CHUNK):
        sl = slice(c * FF_CHUNK, (c + 1) * FF_CHUNK)
        g = jnp.dot(n, wg_ref[:, sl], preferred_element_type=F32)
        u = jnp.dot(n, wu_ref[:, sl], preferred_element_type=F32)
        a = (g * jax.nn.sigmoid(g) * u).astype(BF16)
        acc = acc + jnp.dot(a, wd_ref[sl, :], preferred_element_type=F32)
    return acc


def _ffn_in_kernel(x_ref, g1_ref, wg_ref, wu_ref, wd_ref, g2_ref, win_ref, bin_ref,
                   h_ref, zrkv_ref, zlora_ref, q_ref, kv_ref):
    x = x_ref[...]
    n1 = _rmsnorm(x, g1_ref[...]).astype(BF16)
    h = x + 0.5 * _swiglu(n1, wg_ref, wu_ref, wd_ref)
    h_ref[...] = h
    n2 = _rmsnorm(h, g2_ref[...]).astype(BF16)
    z = jnp.dot(n2, win_ref[...], preferred_element_type=F32) + bin_ref[...]
    o = 0
    for ref, w in ((zrkv_ref, RKV_COLS), (zlora_ref, LORA_PAD), (q_ref, ATTN_WIDTH), (kv_ref, KV_WIDTH)):
        ref[...] = z[:, o:o + w].astype(ref.dtype)
        o += w


def _ffn_in(x, g1, wg, wu, wd, g2, win, b_in, tm):
    rows = x.shape[0]
    row = lambda w: pl.BlockSpec((tm, w), lambda i: (i, 0))
    return pl.pallas_call(
        _ffn_in_kernel,
        grid=(rows // tm,),
        in_specs=[row(D_MODEL), _const_spec((1, D_MODEL)), _const_spec((D_MODEL, D_FF)),
                  _const_spec((D_MODEL, D_FF)), _const_spec((D_FF, D_MODEL)), _const_spec((1, D_MODEL)),
                  _const_spec((D_MODEL, Z_COLS)), _const_spec((1, Z_COLS))],
        out_specs=[row(D_MODEL), row(RKV_COLS), row(LORA_PAD), row(ATTN_WIDTH), row(KV_WIDTH)],
        out_shape=[jax.ShapeDtypeStruct((rows, w), t)
                   for w, t in ((D_MODEL, F32), (RKV_COLS, F32), (LORA_PAD, F32), (ATTN_WIDTH, BF16),
                                (KV_WIDTH, BF16))],
        compiler_params=pltpu.CompilerParams(dimension_semantics=("arbitrary",),
                                             vmem_limit_bytes=VMEM_LIMIT),
        name="ffn_in",
    )(x, g1, wg, wu, wd, g2, win, b_in)


QUAD = 4 * HEAD
N_QUADS = RWKV_WIDTH // QUAD
INV_PASSES = 1
MIX_PASSES = 1
STATE_PASSES = 1


def _blockdiag(x, bd_ref):
    return jnp.concatenate([x * bd_ref[h * HEAD:(h + 1) * HEAD, :] for h in range(4)], axis=0)


def _mm(lhs, rhs, bd_ref, passes=1, nt=False):
    lh = lhs.astype(BF16)
    rh = rhs.astype(BF16)
    if passes == 1:
        l_cat, r_cat = lh, _blockdiag(rh, bd_ref)
    else:
        ll = (lhs - lh.astype(F32)).astype(BF16)
        rl = (rhs - rh.astype(F32)).astype(BF16)
        rh_bd, rl_bd = _blockdiag(rh, bd_ref), _blockdiag(rl, bd_ref)
        l_cat = jnp.concatenate([lh, ll, lh], axis=1)
        r_cat = jnp.concatenate([rh_bd, rh_bd, rl_bd], axis=1 if nt else 0)
    dims = (((1,), (1,)), ((), ())) if nt else (((1,), (0,)), ((), ()))
    return lax.dot_general(l_cat, r_cat, dims, preferred_element_type=F32)


def _mm_tn_diag(lhs_list, rhs_list, head_of_lane):
    lt = jnp.concatenate([l.T for l in lhs_list], axis=1).astype(BF16)
    r = jnp.concatenate(rhs_list, axis=0).astype(BF16)
    full = jnp.dot(lt, r, preferred_element_type=F32)
    out = full[3 * HEAD:4 * HEAD]
    for h in (2, 1, 0):
        out = jnp.where(head_of_lane == h, full[h * HEAD:(h + 1) * HEAD], out)
    return out


def _unit_lower_inverse(a_list, t_idx, j_idx, bd_ref):
    mm = functools.partial(_mm, bd_ref=bd_ref, passes=INV_PASSES)
    same = lambda n: (t_idx // n) == (j_idx // n)
    eye = (t_idx == j_idx).astype(F32)
    a8 = [jnp.where(same(8), a, 0.0) for a in a_list]
    a2 = [mm(x, x) for x in a8]
    a4 = [mm(x, x) for x in a2]
    inv = [eye + x for x in a8]
    inv = [i + mm(i, x) for i, x in zip(inv, a2)]
    inv = [i + mm(i, x) for i, x in zip(inv, a4)]
    for n in (16, 32, 64):
        off = same(n) & jnp.logical_not(same(n // 2))
        ci = [mm(jnp.where(off, a, 0.0), i) for a, i in zip(a_list, inv)]
        inv = [i + mm(i, x) for i, x in zip(inv, ci)]
    return inv


def _shift_lerp(z, prev, mu, row0):
    shifted = jnp.where(row0, prev, pltpu.roll(z, 1, axis=0))
    return z + (shifted - z) * mu


def _group_sum(x, bd_ref):
    return jnp.concatenate(
        [jnp.dot(x[:, q * QUAD:(q + 1) * QUAD].astype(BF16), bd_ref[...], preferred_element_type=F32)
         for q in range(N_QUADS)], axis=1)


def _rwkv_kernel(zrkv_ref, zlora_ref, s0_ref, prev_rkv_ref, prev_lora_ref, mu_rkv_ref, mu_lora_ref,
                 wlora_ref, w0_ref, a0_ref, kk_ref, ka_ref, rk_ref, lnw_ref, lnb_ref, bd_ref,
                 y_ref, sfin_ref, s_scr, prkv_scr, plora_scr, *, nc):
    step = pl.program_id(1)

    @pl.when(step == 0)
    def _():
        s_scr[...] = s0_ref[...]
        prkv_scr[...] = prev_rkv_ref[...]
        plora_scr[...] = prev_lora_ref[...]

    T = CHUNK
    R = nc * T
    zrkv_raw = zrkv_ref[0]
    zlora_raw = zlora_ref[0]
    row0 = lax.broadcasted_iota(jnp.int32, (R, 1), 0) == 0
    zrkv = _shift_lerp(zrkv_raw, prkv_scr[...], mu_rkv_ref[...], row0)
    zlora = _shift_lerp(zlora_raw, plora_scr[...], mu_lora_ref[...], row0)
    prkv_scr[...] = zrkv_raw[R - 1:R, :]
    plora_scr[...] = zlora_raw[R - 1:R, :]

    zr = zrkv[:, 0:RWKV_WIDTH]
    zk = zrkv[:, RWKV_WIDTH:2 * RWKV_WIDTH]
    zv = zrkv[:, 2 * RWKV_WIDTH:3 * RWKV_WIDTH]

    lane = lax.broadcasted_iota(jnp.int32, (R, LORA_PAD), 1)
    act = jnp.where(lane < LORA_W, jnp.tanh(zlora),
                    jnp.where(lane < LORA_W + LORA_A, zlora, jax.nn.sigmoid(zlora)))
    lo = jnp.dot(act.astype(BF16), wlora_ref[...], preferred_element_type=F32)
    lw = -jnp.exp(F32(-0.5)) * jax.nn.sigmoid(w0_ref[...] + lo[:, 0:RWKV_WIDTH])
    a = jax.nn.sigmoid(a0_ref[...] + lo[:, RWKV_WIDTH:2 * RWKV_WIDTH])
    g = lo[:, 2 * RWKV_WIDTH:3 * RWKV_WIDTH]

    kkn = zk * kk_ref[...]
    kk = kkn * jnp.minimum(lax.rsqrt(_group_sum(kkn * kkn, bd_ref)), 1e12)
    k = zk * (1.0 + (a - 1.0) * ka_ref[...])
    b = kk * a
    bonus = _group_sum(zr * k * rk_ref[...], bd_ref) * zv

    t_idx = lax.broadcasted_iota(jnp.int32, (T, QUAD), 0)
    l_idx = lax.broadcasted_iota(jnp.int32, (T, QUAD), 1)
    j_idx = l_idx % HEAD
    head_of_lane = l_idx // HEAD
    strict = j_idx < t_idx
    incl = j_idx <= t_idx
    mm = functools.partial(_mm, bd_ref=bd_ref, passes=MIX_PASSES)

    rr = lax.broadcasted_iota(jnp.int32, (R, R), 0)
    cc = lax.broadcasted_iota(jnp.int32, (R, R), 1)
    tri = jnp.where(((rr // T) == (cc // T)) & (cc <= rr), 1.0, 0.0).astype(BF16)
    lw_hi = lw.astype(BF16)
    lw_lo = (lw - lw_hi.astype(F32)).astype(BF16)
    cum = jnp.dot(tri, lw_hi, preferred_element_type=F32) + jnp.dot(tri, lw_lo, preferred_element_type=F32)
    row_of_chunk = lambda r: jnp.concatenate(
        [jnp.broadcast_to(cum[n * T + r:n * T + r + 1], (T, RWKV_WIDTH)) for n in range(nc)], axis=0)
    mid = row_of_chunk(T // 2 - 1)
    end = row_of_chunk(T - 1)
    e = jnp.exp(cum)
    first = lax.broadcasted_iota(jnp.int32, (R, 1), 0) % T == 0
    e_prev = jnp.where(first, 1.0, pltpu.roll(e, 1, axis=0))
    e_mid = jnp.exp(-mid)
    e_neg = jnp.exp(mid - cum)
    e_end = jnp.exp(end - cum)
    at0 = -kk * e_prev
    rt0 = zr * e
    at = at0 * e_mid
    rt = rt0 * e_mid
    bt = b * e_neg
    kt = k * e_neg
    b_end = b * e_end
    k_end = k * e_end

    probs = [(n, q) for n in range(nc) for q in range(N_QUADS)]
    cut = lambda arr, p: arr[p[0] * T:(p[0] + 1) * T, p[1] * QUAD:(p[1] + 1) * QUAD]
    xs = [jnp.concatenate([cut(at, p), cut(rt, p)], axis=0) for p in probs]
    xb = [mm(x, cut(bt, p), nt=True) for x, p in zip(xs, probs)]
    xk = [mm(x, cut(kt, p), nt=True) for x, p in zip(xs, probs)]
    a_ab = [jnp.where(strict, v[:T], 0.0) for v in xb]
    a_rb = [jnp.where(incl, v[T:], 0.0) for v in xb]
    a_ak = [jnp.where(strict, v[:T], 0.0) for v in xk]
    a_rk = [jnp.where(incl, v[T:], 0.0) for v in xk]
    w1 = [mm(m, cut(zv, p)) for m, p in zip(a_ak, probs)]
    yv = [mm(m, cut(zv, p)) for m, p in zip(a_rk, probs)]
    inv = _unit_lower_inverse(a_ab, t_idx, j_idx, bd_ref)
    au = [mm(i, cut(at0, p)) for i, p in zip(inv, probs)]
    u0 = [mm(i, w) for i, w in zip(inv, w1)]
    r_eff = [cut(rt0, p) + mm(m, x) for m, x, p in zip(a_rb, au, probs)]
    y0 = [v + mm(m, u) for v, m, u in zip(yv, a_rb, u0)]
    m_all = [_mm_tn_diag([x], [cut(b_end, p)], head_of_lane) for x, p in zip(au, probs)]
    g_all = [_mm_tn_diag([u, cut(zv, p)], [cut(b_end, p), cut(k_end, p)], head_of_lane)
             for u, p in zip(u0, probs)]

    mm_s = functools.partial(_mm, bd_ref=bd_ref, passes=STATE_PASSES)
    s = [s_scr[:, q * QUAD:(q + 1) * QUAD] for q in range(N_QUADS)]
    y_rows = []
    for n in range(nc):
        ys = []
        for q in range(N_QUADS):
            i = n * N_QUADS + q
            w_tot = e[(n + 1) * T - 1:(n + 1) * T, q * QUAD:(q + 1) * QUAD]
            s_new = s[q] * w_tot + mm_s(s[q], m_all[i]) + g_all[i]
            ys.append(y0[i] + mm_s(r_eff[i], s[q], nt=True))
            s[q] = s_new
        y_rows.append(jnp.concatenate(ys, axis=1))
    for q in range(N_QUADS):
        s_scr[:, q * QUAD:(q + 1) * QUAD] = s[q]
    y = jnp.concatenate(y_rows, axis=0)
    yc = y - _group_sum(y, bd_ref) * (1.0 / HEAD)
    var = _group_sum(yc * yc, bd_ref) * (1.0 / HEAD)
    yn = yc * lax.rsqrt(var + GN_EPS) * lnw_ref[...] + lnb_ref[...]
    y_ref[0] = ((yn + bonus) * g).astype(y_ref.dtype)

    @pl.when(step == pl.num_programs(1) - 1)
    def _():
        sfin_ref[0] = s_scr[...]


def _rwkv(zrkv, zlora, s0, prev_rkv, prev_lora, p, nc):
    B, T, _ = zrkv.shape
    rows = nc * CHUNK
    blk = lambda w: pl.BlockSpec((1, rows, w), lambda b, c: (b, c, 0))
    vec = lambda w: _const_spec((1, w))
    state = (HEAD, RWKV_WIDTH)
    return pl.pallas_call(
        functools.partial(_rwkv_kernel, nc=nc),
        grid=(B, T // rows),
        in_specs=[blk(RKV_COLS), blk(LORA_PAD), _const_spec(state), vec(RKV_COLS), vec(LORA_PAD),
                  vec(RKV_COLS), vec(LORA_PAD), _const_spec((LORA_PAD, 3 * RWKV_WIDTH))]
                 + [vec(RWKV_WIDTH)] * 7 + [_const_spec((QUAD, QUAD))],
        out_specs=[blk(RWKV_WIDTH), pl.BlockSpec((1,) + state, lambda b, c: (b, 0, 0))],
        out_shape=[jax.ShapeDtypeStruct((B, T, RWKV_WIDTH), BF16),
                   jax.ShapeDtypeStruct((B,) + state, F32)],
        scratch_shapes=[pltpu.VMEM(state, F32), pltpu.VMEM((1, RKV_COLS), F32),
                        pltpu.VMEM((1, LORA_PAD), F32)],
        compiler_params=pltpu.CompilerParams(dimension_semantics=("arbitrary", "arbitrary"),
                                             vmem_limit_bytes=VMEM_LIMIT),
        name="rwkv",
    )(zrkv, zlora, s0, prev_rkv, prev_lora, p["mu_rkv"], p["mu_lora"], p["wlora"], p["w0"], p["a0"],
      p["k_k"], p["k_a"], p["r_k"], p["ln_w"], p["ln_b"], p["bd"])


N_WIN = (WIN_CHUNKS + 1) * CHUNK


KEY_SLOTS = 256
ATTN_CHUNKS_PER_STEP = 4
PAIR = 2 * HEAD
ATTN_HEAD_ORDER = tuple(h for p in range(ATTN_GROUP) for h in (p, ATTN_GROUP + p))


def _attn_bias():
    p = np.arange(ATTN_GROUP)[:, None, None, None]
    i = np.arange(CHUNK)[None, :, None, None]
    j = np.arange(ATTN_KV_HEADS)[None, None, :, None]
    s = np.arange(KEY_SLOTS)[None, None, None, :]
    slope = np.exp2(-8.0 * (ATTN_GROUP * j + p + 1.0) / ATTN_HEADS)
    meta = s >= KEY_SLOTS - N_META
    dist = np.where(meta, N_META + i - (s - (KEY_SLOTS - N_META)), np.abs(WIN_CHUNKS * CHUNK + i - s))
    b0 = np.where((s >= N_WIN) & ~meta, NEG_INF, -slope * dist)
    b1 = np.broadcast_to(np.where(meta, slope * CHUNK, 0.0), b0.shape)
    shape = (ATTN_GROUP * CHUNK, ATTN_KV_HEADS * KEY_SLOTS)
    return (jnp.asarray(b0.reshape(shape), F32), jnp.asarray(b1.reshape(shape), F32))


def _attn_kernel(sink_ref, q_ref, kvp_ref, kvc_ref, kvm_ref, b0_ref, b1_ref, o_ref):
    step = pl.program_id(1)
    nchunks = ATTN_CHUNKS_PER_STEP
    rows = ATTN_GROUP * CHUNK
    kv_rows = jnp.concatenate([kvp_ref[0], kvc_ref[0]], axis=0)
    kvm = kvm_ref[...]
    first_head = lax.broadcasted_iota(jnp.int32, (1, PAIR), 1) < HEAD
    slot = lax.broadcasted_iota(jnp.int32, (1, ATTN_KV_HEADS * KEY_SLOTS), 1) % KEY_SLOTS
    pair_of_row = lax.broadcasted_iota(jnp.int32, (rows, 1), 0) // CHUNK
    sinks = []
    for j in range(ATTN_KV_HEADS):
        col = jnp.zeros((rows, 1), F32)
        for p in range(ATTN_GROUP):
            col = jnp.where(pair_of_row == p, sink_ref[ATTN_GROUP * j + p], col)
        sinks.append(col)

    def per_head(x):
        zero = jnp.zeros_like(x)
        return jnp.concatenate([jnp.where(first_head, x, zero), jnp.where(first_head, zero, x)], axis=0)

    chunk_ids = [step * nchunks + n for n in range(nchunks)]
    keys = [jnp.concatenate([kv_rows[n * CHUNK:n * CHUNK + N_WIN], kvm], axis=0) for n in range(nchunks)]
    qs = [jnp.concatenate([q_ref[0, n * CHUNK:(n + 1) * CHUNK, p * PAIR:(p + 1) * PAIR]
                           for p in range(ATTN_GROUP)], axis=0) * (HEAD ** -0.5) for n in range(nchunks)]
    s = [lax.dot_general(q, per_head(kx[:, :PAIR]), (((1,), (1,)), ((), ())), preferred_element_type=F32)
         for q, kx in zip(qs, keys)]
    s = [x + b0_ref[...] - c.astype(F32) * b1_ref[...] for x, c in zip(s, chunk_ids)]
    s = [jnp.where(slot >= (WIN_CHUNKS - c) * CHUNK, x, NEG_INF) for x, c in zip(s, chunk_ids)]
    outs = []
    for n in range(nchunks):
        ps, inv_den = [], []
        for j in range(ATTN_KV_HEADS):
            sj = s[n][:, j * KEY_SLOTS:(j + 1) * KEY_SLOTS]
            m = jnp.maximum(jnp.max(sj, axis=-1, keepdims=True), sinks[j])
            pj = jnp.exp(sj - m)
            inv_den.append(1.0 / (jnp.sum(pj, axis=-1, keepdims=True) + jnp.exp(sinks[j] - m)))
            ps.append(pj.astype(BF16))
        o = jnp.dot(jnp.concatenate(ps, axis=1), per_head(keys[n][:, PAIR:]),
                    preferred_element_type=F32)
        o = o * jnp.where(first_head, inv_den[0], inv_den[1])
        outs.append(jnp.concatenate([o[p * CHUNK:(p + 1) * CHUNK] for p in range(ATTN_GROUP)], axis=1))
    o_ref[0] = jnp.concatenate(outs, axis=0).astype(o_ref.dtype)


def _attn(q, kv, kv_meta, sinks):
    B, T, _ = q.shape
    rows = ATTN_CHUNKS_PER_STEP * CHUNK
    back = WIN_CHUNKS * CHUNK
    b0, b1 = _attn_bias()
    return pl.pallas_call(
        _attn_kernel,
        grid=(B, T // rows),
        in_specs=[pl.BlockSpec(memory_space=pltpu.SMEM),
                  pl.BlockSpec((1, rows, ATTN_WIDTH), lambda b, c: (b, c, 0)),
                  pl.BlockSpec((1, back, KV_WIDTH),
                               lambda b, c: (b, jnp.maximum(c * (rows // back) - 1, 0), 0)),
                  pl.BlockSpec((1, rows, KV_WIDTH), lambda b, c: (b, c, 0)),
                  _const_spec((CHUNK, KV_WIDTH)), _const_spec(b0.shape), _const_spec(b1.shape)],
        out_specs=pl.BlockSpec((1, rows, ATTN_WIDTH), lambda b, c: (b, c, 0)),
        out_shape=jax.ShapeDtypeStruct((B, T, ATTN_WIDTH), BF16),
        compiler_params=pltpu.CompilerParams(dimension_semantics=("arbitrary", "arbitrary"),
                                             vmem_limit_bytes=VMEM_LIMIT),
        name="attn",
    )(sinks, q, kv, kv, kv_meta, b0, b1)


def _out_ffn_kernel(h_ref, yr_ref, ya_ref, wo_ref, g_ref, wg_ref, wu_ref, wd_ref, gf_ref, o_ref):
    y = jnp.concatenate([yr_ref[...], ya_ref[...]], axis=-1)
    h = h_ref[...] + jnp.dot(y, wo_ref[...], preferred_element_type=F32)
    n = _rmsnorm(h, g_ref[...]).astype(BF16)
    h = h + 0.5 * _swiglu(n, wg_ref, wu_ref, wd_ref)
    o_ref[...] = _rmsnorm(h, gf_ref[...])


def _out_ffn(h, y_rwkv, y_attn, wo, g, wg, wu, wd, gf, tm):
    rows = h.shape[0]
    row = lambda w: pl.BlockSpec((tm, w), lambda i: (i, 0))
    return pl.pallas_call(
        _out_ffn_kernel,
        grid=(rows // tm,),
        in_specs=[row(D_MODEL), row(RWKV_WIDTH), row(ATTN_WIDTH), _const_spec((D_MODEL, D_MODEL)),
                  _const_spec((1, D_MODEL)), _const_spec((D_MODEL, D_FF)), _const_spec((D_MODEL, D_FF)),
                  _const_spec((D_FF, D_MODEL)), _const_spec((1, D_MODEL))],
        out_specs=row(D_MODEL),
        out_shape=jax.ShapeDtypeStruct((rows, D_MODEL), F32),
        compiler_params=pltpu.CompilerParams(dimension_semantics=("arbitrary",),
                                             vmem_limit_bytes=VMEM_LIMIT),
        name="out_ffn",
    )(h, y_rwkv, y_attn, wo, g, wg, wu, wd, gf)


def _pack_w_in(w_in, b_attn):
    rwkv_cols = RKV_COLS + LORA_COLS
    pad = jnp.zeros((D_MODEL, LORA_PAD - LORA_COLS), w_in.dtype)
    attn_cols = np.concatenate([_head_cols(ATTN_HEAD_ORDER), np.arange(ATTN_WIDTH, ATTN_WIDTH + KV_WIDTH)])
    w = jnp.concatenate([w_in[:, :rwkv_cols], pad, w_in[:, rwkv_cols:][:, attn_cols]], axis=1).astype(BF16)
    b = jnp.concatenate([jnp.zeros((RKV_COLS + LORA_PAD,), F32), b_attn.astype(F32)[attn_cols]])[None]
    return w, b


def _head_cols(order):
    return np.concatenate([np.arange(h * HEAD, (h + 1) * HEAD) for h in order])


def _pack_lora(w2, a2, g2):
    w = jnp.zeros((LORA_PAD, 3 * RWKV_WIDTH), F32)
    w = w.at[0:LORA_W, 0:RWKV_WIDTH].set(w2)
    w = w.at[LORA_W:LORA_W + LORA_A, RWKV_WIDTH:2 * RWKV_WIDTH].set(a2)
    w = w.at[LORA_W + LORA_A:LORA_COLS, 2 * RWKV_WIDTH:].set(g2)
    return w


def _block_ones():
    i = jnp.arange(QUAD) // HEAD
    return (i[:, None] == i[None, :]).astype(BF16)


def kernel(x, meta_tokens, ffn1_norm, ffn1_w_gate, ffn1_w_up, ffn1_w_down, mix_norm, w_in, b_attn, rwkv_mu, rwkv_w0, rwkv_w2, rwkv_a0, rwkv_a2, rwkv_g2, rwkv_k_k, rwkv_k_a, rwkv_r_k, rwkv_ln_w, rwkv_ln_b, attn_sinks, w_out, ffn2_norm, ffn2_w_gate, ffn2_w_up, ffn2_w_down, final_norm):
    assert ffn1_norm.shape[0] == 1, "single-layer trunk"
    B, T, D = x.shape
    row = lambda v: v.reshape(1, -1).astype(F32)
    bf = lambda w: w[0].astype(BF16)

    win, b_in = _pack_w_in(w_in[0], b_attn[0])
    mu = rwkv_mu[0].astype(F32)
    mu_lora = jnp.concatenate([mu[RKV_COLS:], jnp.zeros((LORA_PAD - LORA_COLS,), F32)])
    rp = dict(mu_rkv=row(mu[:RKV_COLS]), mu_lora=row(mu_lora),
              wlora=_pack_lora(rwkv_w2[0], rwkv_a2[0], rwkv_g2[0]).astype(BF16), bd=_block_ones(),
              w0=row(rwkv_w0[0]), a0=row(rwkv_a0[0]), k_k=row(rwkv_k_k[0]), k_a=row(rwkv_k_a[0]),
              r_k=row(rwkv_r_k[0]), ln_w=row(rwkv_ln_w[0]), ln_b=row(rwkv_ln_b[0]))
    ffn1 = (row(ffn1_norm[0]), bf(ffn1_w_gate), bf(ffn1_w_up), bf(ffn1_w_down))

    xm = jnp.concatenate([jnp.zeros((CHUNK - N_META, D), F32), meta_tokens.astype(F32)], axis=0)
    _, zrkv_m, zlora_m, _, kv_m = _ffn_in(xm, *ffn1, row(mix_norm[0]), win, b_in, tm=CHUNK)
    zeros_state = jnp.zeros((HEAD, RWKV_WIDTH), F32)
    _, s_meta = _rwkv(zrkv_m[None], zlora_m[None], zeros_state,
                      jnp.zeros((1, RKV_COLS), F32), jnp.zeros((1, LORA_PAD), F32), rp, nc=1)

    xf = x.reshape(B * T, D)
    h1, zrkv, zlora, q, kv = _ffn_in(xf, *ffn1, row(mix_norm[0]), win, b_in, tm=FFN_ROWS)
    y_rwkv, _ = _rwkv(zrkv.reshape(B, T, -1), zlora.reshape(B, T, -1), s_meta[0],
                      zrkv_m[CHUNK - 1:], zlora_m[CHUNK - 1:], rp, nc=RWKV_CHUNKS_PER_STEP)
    y_attn = _attn(q.reshape(B, T, -1), kv.reshape(B, T, -1), kv_m, attn_sinks[0].astype(F32))
    wo_rows = np.concatenate([np.arange(RWKV_WIDTH), RWKV_WIDTH + _head_cols(ATTN_HEAD_ORDER)])
    out = _out_ffn(h1, y_rwkv.reshape(B * T, -1), y_attn.reshape(B * T, -1), w_out[0][wo_rows].astype(BF16),
                   row(ffn2_norm[0]), bf(ffn2_w_gate), bf(ffn2_w_up), bf(ffn2_w_down),
                   row(final_norm), tm=FFN_ROWS)
    return out.reshape(B, T, D)
```

```python
import functools

import jax
import jax.numpy as jnp
import numpy as np
from jax import lax
from jax.experimental import pallas as pl
from jax.experimental.pallas import tpu as pltpu

F32 = jnp.float32
BF16 = jnp.bfloat16

D_MODEL = 1024
D_FF = 2816
N_META = 16
NORM_EPS = 1e-5
CHUNK = 64

HEAD = 64
RWKV_HEADS = 8
RWKV_WIDTH = 512
LORA_W, LORA_A, LORA_G = 32, 32, 96
LORA_COLS = LORA_W + LORA_A + LORA_G
LORA_PAD = 256
GN_EPS = 64e-5

ATTN_HEADS = 8
ATTN_KV_HEADS = 2
ATTN_GROUP = 4
ATTN_WIDTH = 512
KV_WIDTH = 2 * ATTN_KV_HEADS * HEAD
WIN_CHUNKS = 2
NEG_INF = -1e30

RKV_COLS = 3 * RWKV_WIDTH
Z_COLS = RKV_COLS + LORA_PAD + ATTN_WIDTH + KV_WIDTH

RWKV_CHUNKS_PER_STEP = 8
FFN_ROWS = 512
FF_CHUNK = 256
VMEM_LIMIT = 56 * 1024 * 1024


def _const_spec(shape):
    nd = len(shape)
    return pl.BlockSpec(shape, lambda *_: (0,) * nd, pipeline_mode=pl.Buffered(1))


def _rmsnorm(x, g):
    return x * lax.rsqrt(jnp.mean(x * x, axis=-1, keepdims=True) + NORM_EPS) * g


def _drain(gen):
    while True:
        try:
            next(gen)
        except StopIteration as stop:
            return stop.value


def _swiglu(n, wg_ref, wu_ref, wd_ref):
    acc = jnp.zeros((n.shape[0], D_MODEL), F32)
    for c in range(D_FF // FF_CHUNK):
        sl = slice(c * FF_CHUNK, (c + 1) * FF_CHUNK)
        g = jnp.dot(n, wg_ref[:, sl], preferred_element_type=F32)
        u = jnp.dot(n, wu_ref[:, sl], preferred_element_type=F32)
        a = (g * jax.nn.sigmoid(g) * u).astype(BF16)
        acc = acc + jnp.dot(a, wd_ref[sl, :], preferred_element_type=F32)
    return acc


def _ffn_in_kernel(x_ref, g1_ref, wg_ref, wu_ref, wd_ref, g2_ref, win_ref, bin_ref,
                   h_ref, zrkv_ref, zlora_ref, q_ref, kv_ref):
    x = x_ref[...]
    n1 = _rmsnorm(x, g1_ref[...]).astype(BF16)
    h = x + 0.5 * _swiglu(n1, wg_ref, wu_ref, wd_ref)
    h_ref[...] = h
    n2 = _rmsnorm(h, g2_ref[...]).astype(BF16)
    z = jnp.dot(n2, win_ref[...], preferred_element_type=F32) + bin_ref[...]
    o = 0
    for ref, w in ((zrkv_ref, RKV_COLS), (zlora_ref, LORA_PAD), (q_ref, ATTN_WIDTH), (kv_ref, KV_WIDTH)):
        ref[...] = z[:, o:o + w].astype(ref.dtype)
        o += w


def _ffn_in(x, g1, wg, wu, wd, g2, win, b_in, tm):
    rows = x.shape[0]
    row = lambda w: pl.BlockSpec((tm, w), lambda i: (i, 0))
    return pl.pallas_call(
        _ffn_in_kernel,
        grid=(rows // tm,),
        in_specs=[row(D_MODEL), _const_spec((1, D_MODEL)), _const_spec((D_MODEL, D_FF)),
                  _const_spec((D_MODEL, D_FF)), _const_spec((D_FF, D_MODEL)), _const_spec((1, D_MODEL)),
                  _const_spec((D_MODEL, Z_COLS)), _const_spec((1, Z_COLS))],
        out_specs=[row(D_MODEL), row(RKV_COLS), row(LORA_PAD), row(ATTN_WIDTH), row(KV_WIDTH)],
        out_shape=[jax.ShapeDtypeStruct((rows, w), t)
                   for w, t in ((D_MODEL, F32), (RKV_COLS, F32), (LORA_PAD, F32), (ATTN_WIDTH, BF16),
                                (KV_WIDTH, BF16))],
        compiler_params=pltpu.CompilerParams(dimension_semantics=("arbitrary",),
                                             vmem_limit_bytes=VMEM_LIMIT),
        name="ffn_in",
    )(x, g1, wg, wu, wd, g2, win, b_in)


QUAD = 4 * HEAD
N_QUADS = RWKV_WIDTH // QUAD


def _blockdiag(x, bd_ref):
    return jnp.concatenate([x * bd_ref[h * HEAD:(h + 1) * HEAD, :] for h in range(4)], axis=0)


def _mm(lhs, rhs, bd_ref, nt=False):
    many = isinstance(lhs, (list, tuple))
    l = jnp.concatenate([x.astype(BF16) for x in lhs], axis=0) if many else lhs.astype(BF16)
    dims = (((1,), (1,)), ((), ())) if nt else (((1,), (0,)), ((), ()))
    out = lax.dot_general(l, _blockdiag(rhs.astype(BF16), bd_ref), dims, preferred_element_type=F32)
    return [out[i * CHUNK:(i + 1) * CHUNK] for i in range(len(lhs))] if many else out


def _mm_tn_diag(lhs_list, rhs_list, head_of_lane):
    lt = jnp.concatenate([l.T for l in lhs_list], axis=1).astype(BF16)
    r = jnp.concatenate(rhs_list, axis=0).astype(BF16)
    full = jnp.dot(lt, r, preferred_element_type=F32)
    out = full[3 * HEAD:4 * HEAD]
    for h in (2, 1, 0):
        out = jnp.where(head_of_lane == h, full[h * HEAD:(h + 1) * HEAD], out)
    return out


def _unit_lower_inverse(a_list, t_idx, j_idx, bd_ref):
    mm = functools.partial(_mm, bd_ref=bd_ref)
    same = lambda n: (t_idx // n) == (j_idx // n)
    eye = (t_idx == j_idx).astype(F32)
    a8 = [jnp.where(same(8), a, 0.0) for a in a_list]
    a2 = [mm(x, x) for x in a8]
    yield
    inv = [eye + x for x in a8]
    both = [mm([i, x], x) for i, x in zip(inv, a2)]
    inv = [i + t[0] for i, t in zip(inv, both)]
    yield
    inv = [i + mm(i, t[1]) for i, t in zip(inv, both)]
    yield
    for n in (16, 32, 64):
        off = same(n) & jnp.logical_not(same(n // 2))
        ci = [mm(jnp.where(off, a, 0.0), i) for a, i in zip(a_list, inv)]
        yield
        inv = [i + mm(i, x) for i, x in zip(inv, ci)]
        yield
    return inv


def _shift_lerp(z, prev, mu, row0):
    shifted = jnp.where(row0, prev, pltpu.roll(z, 1, axis=0))
    return z + (shifted - z) * mu


def _group_sum(x, bd_ref):
    return jnp.concatenate(
        [jnp.dot(x[:, q * QUAD:(q + 1) * QUAD].astype(BF16), bd_ref[...], preferred_element_type=F32)
         for q in range(N_QUADS)], axis=1)


N_RWKV_PARAMS = 11


def _rwkv_stages(zrkv_raw, zlora_raw, params, s_scr, prkv_scr, plora_scr, store_y, nc):
    (mu_rkv_ref, mu_lora_ref, wlora_ref, w0_ref, a0_ref, kk_ref, ka_ref, rk_ref, lnw_ref, lnb_ref,
     bd_ref) = params
    T = CHUNK
    R = nc * T
    row0 = lax.broadcasted_iota(jnp.int32, (R, 1), 0) == 0
    zrkv = _shift_lerp(zrkv_raw, prkv_scr[...], mu_rkv_ref[...], row0)
    zlora = _shift_lerp(zlora_raw, plora_scr[...], mu_lora_ref[...], row0)
    prkv_scr[...] = zrkv_raw[R - 1:R, :]
    plora_scr[...] = zlora_raw[R - 1:R, :]

    zr = zrkv[:, 0:RWKV_WIDTH]
    zk = zrkv[:, RWKV_WIDTH:2 * RWKV_WIDTH]
    zv = zrkv[:, 2 * RWKV_WIDTH:3 * RWKV_WIDTH]

    lane = lax.broadcasted_iota(jnp.int32, (R, LORA_PAD), 1)
    act = jnp.where(lane < LORA_W, jnp.tanh(zlora),
                    jnp.where(lane < LORA_W + LORA_A, zlora, jax.nn.sigmoid(zlora)))
    lo = jnp.dot(act.astype(BF16), wlora_ref[...], preferred_element_type=F32)
    lw = -jnp.exp(F32(-0.5)) * jax.nn.sigmoid(w0_ref[...] + lo[:, 0:RWKV_WIDTH])
    a = jax.nn.sigmoid(a0_ref[...] + lo[:, RWKV_WIDTH:2 * RWKV_WIDTH])
    g = lo[:, 2 * RWKV_WIDTH:3 * RWKV_WIDTH]

    kkn = zk * kk_ref[...]
    kk = kkn * jnp.minimum(lax.rsqrt(_group_sum(kkn * kkn, bd_ref)), 1e12)
    k = zk * (1.0 + (a - 1.0) * ka_ref[...])
    b = kk * a
    bonus = _group_sum(zr * k * rk_ref[...], bd_ref) * zv
    yield

    t_idx = lax.broadcasted_iota(jnp.int32, (T, QUAD), 0)
    l_idx = lax.broadcasted_iota(jnp.int32, (T, QUAD), 1)
    j_idx = l_idx % HEAD
    head_of_lane = l_idx // HEAD
    strict = j_idx < t_idx
    incl = j_idx <= t_idx
    mm = functools.partial(_mm, bd_ref=bd_ref)

    rr = lax.broadcasted_iota(jnp.int32, (R, R), 0)
    cc = lax.broadcasted_iota(jnp.int32, (R, R), 1)
    tri = jnp.where(((rr // T) == (cc // T)) & (cc <= rr), 1.0, 0.0).astype(BF16)
    lw_hi = lw.astype(BF16)
    lw_lo = (lw - lw_hi.astype(F32)).astype(BF16)
    cum = jnp.dot(tri, lw_hi, preferred_element_type=F32) + jnp.dot(tri, lw_lo, preferred_element_type=F32)
    row_of_chunk = lambda r: jnp.concatenate(
        [jnp.broadcast_to(cum[n * T + r:n * T + r + 1], (T, RWKV_WIDTH)) for n in range(nc)], axis=0)
    mid = row_of_chunk(T // 2 - 1)
    end = row_of_chunk(T - 1)
    e = jnp.exp(cum)
    first = lax.broadcasted_iota(jnp.int32, (R, 1), 0) % T == 0
    e_prev = jnp.where(first, 1.0, pltpu.roll(e, 1, axis=0))
    e_mid = jnp.exp(-mid)
    e_neg = jnp.exp(mid - cum)
    e_end = jnp.exp(end - cum)
    at0 = -kk * e_prev
    rt0 = zr * e
    at = at0 * e_mid
    rt = rt0 * e_mid
    bt = b * e_neg
    kt = k * e_neg
    b_end = b * e_end
    k_end = k * e_end
    yield

    probs = [(n, q) for n in range(nc) for q in range(N_QUADS)]
    cut = lambda arr, p: arr[p[0] * T:(p[0] + 1) * T, p[1] * QUAD:(p[1] + 1) * QUAD]
    xs = [jnp.concatenate([cut(at, p), cut(rt, p)], axis=0) for p in probs]
    xb = [mm(x, cut(bt, p), nt=True) for x, p in zip(xs, probs)]
    yield
    xk = [mm(x, cut(kt, p), nt=True) for x, p in zip(xs, probs)]
    a_ab = [jnp.where(strict, v[:T], 0.0) for v in xb]
    a_rb = [jnp.where(incl, v[T:], 0.0) for v in xb]
    a_ak = [jnp.where(strict, v[:T], 0.0) for v in xk]
    a_rk = [jnp.where(incl, v[T:], 0.0) for v in xk]
    yield
    w1yv = [mm([m1, m2], cut(zv, p)) for m1, m2, p in zip(a_ak, a_rk, probs)]
    w1 = [t[0] for t in w1yv]
    yv = [t[1] for t in w1yv]
    yield
    inv = yield from _unit_lower_inverse(a_ab, t_idx, j_idx, bd_ref)
    au = [mm(i, cut(at0, p)) for i, p in zip(inv, probs)]
    u0 = [mm(i, w) for i, w in zip(inv, w1)]
    yield
    r_eff = [cut(rt0, p) + mm(m, x) for m, x, p in zip(a_rb, au, probs)]
    y0 = [v + mm(m, u) for v, m, u in zip(yv, a_rb, u0)]
    yield
    m_all = [_mm_tn_diag([x], [cut(b_end, p)], head_of_lane) for x, p in zip(au, probs)]
    yield
    g_all = [_mm_tn_diag([u, cut(zv, p)], [cut(b_end, p), cut(k_end, p)], head_of_lane)
             for u, p in zip(u0, probs)]
    yield

    s = [s_scr[:, q * QUAD:(q + 1) * QUAD] for q in range(N_QUADS)]
    y_rows = []
    for n in range(nc):
        ys = []
        for q in range(N_QUADS):
            i = n * N_QUADS + q
            w_tot = e[(n + 1) * T - 1:(n + 1) * T, q * QUAD:(q + 1) * QUAD]
            s_new = s[q] * w_tot + mm(s[q], m_all[i]) + g_all[i]
            ys.append(y0[i] + mm(r_eff[i], s[q], nt=True))
            s[q] = s_new
        y_rows.append(jnp.concatenate(ys, axis=1))
        yield
    for q in range(N_QUADS):
        s_scr[:, q * QUAD:(q + 1) * QUAD] = s[q]
    y = jnp.concatenate(y_rows, axis=0)
    yc = y - _group_sum(y, bd_ref) * (1.0 / HEAD)
    var = _group_sum(yc * yc, bd_ref) * (1.0 / HEAD)
    yn = yc * lax.rsqrt(var + GN_EPS) * lnw_ref[...] + lnb_ref[...]
    store_y((yn + bonus) * g)


def _rwkv_kernel(zrkv_ref, zlora_ref, s0_ref, prev_rkv_ref, prev_lora_ref, *rest, nc):
    params, (y_ref, sfin_ref, s_scr, prkv_scr, plora_scr) = rest[:N_RWKV_PARAMS], rest[N_RWKV_PARAMS:]
    step = pl.program_id(1)

    @pl.when(step == 0)
    def _():
        s_scr[...] = s0_ref[...]
        prkv_scr[...] = prev_rkv_ref[...]
        plora_scr[...] = prev_lora_ref[...]

    def store_y(y):
        y_ref[0] = y.astype(y_ref.dtype)

    for _ in _rwkv_stages(zrkv_ref[0], zlora_ref[0], params, s_scr, prkv_scr, plora_scr, store_y, nc):
        pass

    @pl.when(step == pl.num_programs(1) - 1)
    def _():
        sfin_ref[0] = s_scr[...]


def _rwkv(zrkv, zlora, s0, prev_rkv, prev_lora, p, nc):
    B, T, _ = zrkv.shape
    rows = nc * CHUNK
    blk = lambda w: pl.BlockSpec((1, rows, w), lambda b, c: (b, c, 0))
    vec = lambda w: _const_spec((1, w))
    state = (HEAD, RWKV_WIDTH)
    return pl.pallas_call(
        functools.partial(_rwkv_kernel, nc=nc),
        grid=(B, T // rows),
        in_specs=[blk(RKV_COLS), blk(LORA_PAD), _const_spec(state), vec(RKV_COLS), vec(LORA_PAD),
                  vec(RKV_COLS), vec(LORA_PAD), _const_spec((LORA_PAD, 3 * RWKV_WIDTH))]
                 + [vec(RWKV_WIDTH)] * 7 + [_const_spec((QUAD, QUAD))],
        out_specs=[blk(RWKV_WIDTH), pl.BlockSpec((1,) + state, lambda b, c: (b, 0, 0))],
        out_shape=[jax.ShapeDtypeStruct((B, T, RWKV_WIDTH), BF16),
                   jax.ShapeDtypeStruct((B,) + state, F32)],
        scratch_shapes=[pltpu.VMEM(state, F32), pltpu.VMEM((1, RKV_COLS), F32),
                        pltpu.VMEM((1, LORA_PAD), F32)],
        compiler_params=pltpu.CompilerParams(dimension_semantics=("arbitrary", "arbitrary"),
                                             vmem_limit_bytes=VMEM_LIMIT),
        name="rwkv",
    )(zrkv, zlora, s0, prev_rkv, prev_lora, p["mu_rkv"], p["mu_lora"], p["wlora"], p["w0"], p["a0"],
      p["k_k"], p["k_a"], p["r_k"], p["ln_w"], p["ln_b"], p["bd"])


N_WIN = (WIN_CHUNKS + 1) * CHUNK


KEY_SLOTS = 256
ATTN_CHUNKS_PER_STEP = 4
PAIR = 2 * HEAD
ATTN_HEAD_ORDER = tuple(h for p in range(ATTN_GROUP) for h in (p, ATTN_GROUP + p))


def _attn_bias():
    p = np.arange(ATTN_GROUP)[:, None, None, None]
    i = np.arange(CHUNK)[None, :, None, None]
    j = np.arange(ATTN_KV_HEADS)[None, None, :, None]
    s = np.arange(KEY_SLOTS)[None, None, None, :]
    slope = np.exp2(-8.0 * (ATTN_GROUP * j + p + 1.0) / ATTN_HEADS)
    meta = s >= KEY_SLOTS - N_META
    dist = np.where(meta, N_META + i - (s - (KEY_SLOTS - N_META)), np.abs(WIN_CHUNKS * CHUNK + i - s))
    b0 = np.where((s >= N_WIN) & ~meta, NEG_INF, -slope * dist)
    b1 = np.broadcast_to(np.where(meta, slope * CHUNK, 0.0), b0.shape)
    shape = (ATTN_GROUP * CHUNK, ATTN_KV_HEADS * KEY_SLOTS)
    return (jnp.asarray(b0.reshape(shape), F32), jnp.asarray(b1.reshape(shape), F32))


def _attn_kernel(sink_ref, q_ref, kvp_ref, kvc_ref, kvm_ref, b0_ref, b1_ref, o_ref):
    step = pl.program_id(1)
    nchunks = ATTN_CHUNKS_PER_STEP
    rows = ATTN_GROUP * CHUNK
    kv_rows = jnp.concatenate([kvp_ref[0], kvc_ref[0]], axis=0)
    kvm = kvm_ref[...]
    first_head = lax.broadcasted_iota(jnp.int32, (1, PAIR), 1) < HEAD
    slot = lax.broadcasted_iota(jnp.int32, (1, ATTN_KV_HEADS * KEY_SLOTS), 1) % KEY_SLOTS
    pair_of_row = lax.broadcasted_iota(jnp.int32, (rows, 1), 0) // CHUNK
    sinks = []
    for j in range(ATTN_KV_HEADS):
        col = jnp.zeros((rows, 1), F32)
        for p in range(ATTN_GROUP):
            col = jnp.where(pair_of_row == p, sink_ref[ATTN_GROUP * j + p], col)
        sinks.append(col)

    def per_head(x):
        zero = jnp.zeros_like(x)
        return jnp.concatenate([jnp.where(first_head, x, zero), jnp.where(first_head, zero, x)], axis=0)

    chunk_ids = [step * nchunks + n for n in range(nchunks)]
    keys = [jnp.concatenate([kv_rows[n * CHUNK:n * CHUNK + N_WIN], kvm], axis=0) for n in range(nchunks)]
    qs = [jnp.concatenate([q_ref[0, n * CHUNK:(n + 1) * CHUNK, p * PAIR:(p + 1) * PAIR]
                           for p in range(ATTN_GROUP)], axis=0) * (HEAD ** -0.5) for n in range(nchunks)]
    s = [lax.dot_general(q, per_head(kx[:, :PAIR]), (((1,), (1,)), ((), ())), preferred_element_type=F32)
         for q, kx in zip(qs, keys)]
    s = [x + b0_ref[...] - c.astype(F32) * b1_ref[...] for x, c in zip(s, chunk_ids)]
    s = [jnp.where(slot >= (WIN_CHUNKS - c) * CHUNK, x, NEG_INF) for x, c in zip(s, chunk_ids)]
    outs = []
    for n in range(nchunks):
        ps, inv_den = [], []
        for j in range(ATTN_KV_HEADS):
            sj = s[n][:, j * KEY_SLOTS:(j + 1) * KEY_SLOTS]
            m = jnp.maximum(jnp.max(sj, axis=-1, keepdims=True), sinks[j])
            pj = jnp.exp(sj - m)
            inv_den.append(1.0 / (jnp.sum(pj, axis=-1, keepdims=True) + jnp.exp(sinks[j] - m)))
            ps.append(pj.astype(BF16))
        o = jnp.dot(jnp.concatenate(ps, axis=1), per_head(keys[n][:, PAIR:]),
                    preferred_element_type=F32)
        o = o * jnp.where(first_head, inv_den[0], inv_den[1])
        outs.append(jnp.concatenate([o[p * CHUNK:(p + 1) * CHUNK] for p in range(ATTN_GROUP)], axis=1))
    o_ref[0] = jnp.concatenate(outs, axis=0).astype(o_ref.dtype)


def _attn(q, kv, kv_meta, sinks):
    B, T, _ = q.shape
    rows = ATTN_CHUNKS_PER_STEP * CHUNK
    back = WIN_CHUNKS * CHUNK
    b0, b1 = _attn_bias()
    return pl.pallas_call(
        _attn_kernel,
        grid=(B, T // rows),
        in_specs=[pl.BlockSpec(memory_space=pltpu.SMEM),
                  pl.BlockSpec((1, rows, ATTN_WIDTH), lambda b, c: (b, c, 0)),
                  pl.BlockSpec((1, back, KV_WIDTH),
                               lambda b, c: (b, jnp.maximum(c * (rows // back) - 1, 0), 0)),
                  pl.BlockSpec((1, rows, KV_WIDTH), lambda b, c: (b, c, 0)),
                  _const_spec((CHUNK, KV_WIDTH)), _const_spec(b0.shape), _const_spec(b1.shape)],
        out_specs=pl.BlockSpec((1, rows, ATTN_WIDTH), lambda b, c: (b, c, 0)),
        out_shape=jax.ShapeDtypeStruct((B, T, ATTN_WIDTH), BF16),
        compiler_params=pltpu.CompilerParams(dimension_semantics=("arbitrary", "arbitrary"),
                                             vmem_limit_bytes=VMEM_LIMIT),
        name="attn",
    )(sinks, q, kv, kv, kv_meta, b0, b1)


def _out_ffn_kernel(h_ref, yr_ref, ya_ref, wo_ref, g_ref, wg_ref, wu_ref, wd_ref, gf_ref, o_ref):
    y = jnp.concatenate([yr_ref[...], ya_ref[...]], axis=-1)
    h = h_ref[...] + jnp.dot(y, wo_ref[...], preferred_element_type=F32)
    n = _rmsnorm(h, g_ref[...]).astype(BF16)
    h = h + 0.5 * _swiglu(n, wg_ref, wu_ref, wd_ref)
    o_ref[...] = _rmsnorm(h, gf_ref[...])


def _out_ffn(h, y_rwkv, y_attn, wo, g, wg, wu, wd, gf, tm):
    rows = h.shape[0]
    row = lambda w: pl.BlockSpec((tm, w), lambda i: (i, 0))
    return pl.pallas_call(
        _out_ffn_kernel,
        grid=(rows // tm,),
        in_specs=[row(D_MODEL), row(RWKV_WIDTH), row(ATTN_WIDTH), _const_spec((D_MODEL, D_MODEL)),
                  _const_spec((1, D_MODEL)), _const_spec((D_MODEL, D_FF)), _const_spec((D_MODEL, D_FF)),
                  _const_spec((D_FF, D_MODEL)), _const_spec((1, D_MODEL))],
        out_specs=row(D_MODEL),
        out_shape=jax.ShapeDtypeStruct((rows, D_MODEL), F32),
        compiler_params=pltpu.CompilerParams(dimension_semantics=("arbitrary",),
                                             vmem_limit_bytes=VMEM_LIMIT),
        name="out_ffn",
    )(h, y_rwkv, y_attn, wo, g, wg, wu, wd, gf)


def _pack_w_in(w_in, b_attn):
    rwkv_cols = RKV_COLS + LORA_COLS
    pad = jnp.zeros((D_MODEL, LORA_PAD - LORA_COLS), w_in.dtype)
    attn_cols = np.concatenate([_head_cols(ATTN_HEAD_ORDER), np.arange(ATTN_WIDTH, ATTN_WIDTH + KV_WIDTH)])
    w = jnp.concatenate([w_in[:, :rwkv_cols], pad, w_in[:, rwkv_cols:][:, attn_cols]], axis=1).astype(BF16)
    b = jnp.concatenate([jnp.zeros((RKV_COLS + LORA_PAD,), F32), b_attn.astype(F32)[attn_cols]])[None]
    return w, b


def _head_cols(order):
    return np.concatenate([np.arange(h * HEAD, (h + 1) * HEAD) for h in order])


def _pack_lora(w2, a2, g2):
    w = jnp.zeros((LORA_PAD, 3 * RWKV_WIDTH), F32)
    w = w.at[0:LORA_W, 0:RWKV_WIDTH].set(w2)
    w = w.at[LORA_W:LORA_W + LORA_A, RWKV_WIDTH:2 * RWKV_WIDTH].set(a2)
    w = w.at[LORA_W + LORA_A:LORA_COLS, 2 * RWKV_WIDTH:].set(g2)
    return w


def _block_ones():
    i = jnp.arange(QUAD) // HEAD
    return (i[:, None] == i[None, :]).astype(BF16)


def kernel(x, meta_tokens, ffn1_norm, ffn1_w_gate, ffn1_w_up, ffn1_w_down, mix_norm, w_in, b_attn, rwkv_mu, rwkv_w0, rwkv_w2, rwkv_a0, rwkv_a2, rwkv_g2, rwkv_k_k, rwkv_k_a, rwkv_r_k, rwkv_ln_w, rwkv_ln_b, attn_sinks, w_out, ffn2_norm, ffn2_w_gate, ffn2_w_up, ffn2_w_down, final_norm):
    assert ffn1_norm.shape[0] == 1, "single-layer trunk"
    B, T, D = x.shape
    row = lambda v: v.reshape(1, -1).astype(F32)
    bf = lambda w: w[0].astype(BF16)

    win, b_in = _pack_w_in(w_in[0], b_attn[0])
    mu = rwkv_mu[0].astype(F32)
    mu_lora = jnp.concatenate([mu[RKV_COLS:], jnp.zeros((LORA_PAD - LORA_COLS,), F32)])
    rp = dict(mu_rkv=row(mu[:RKV_COLS]), mu_lora=row(mu_lora),
              wlora=_pack_lora(rwkv_w2[0], rwkv_a2[0], rwkv_g2[0]).astype(BF16), bd=_block_ones(),
              w0=row(rwkv_w0[0]), a0=row(rwkv_a0[0]), k_k=row(rwkv_k_k[0]), k_a=row(rwkv_k_a[0]),
              r_k=row(rwkv_r_k[0]), ln_w=row(rwkv_ln_w[0]), ln_b=row(rwkv_ln_b[0]))
    ffn1 = (row(ffn1_norm[0]), bf(ffn1_w_gate), bf(ffn1_w_up), bf(ffn1_w_down))

    xm = jnp.concatenate([jnp.zeros((CHUNK - N_META, D), F32), meta_tokens.astype(F32)], axis=0)
    _, zrkv_m, zlora_m, _, kv_m = _ffn_in(xm, *ffn1, row(mix_norm[0]), win, b_in, tm=CHUNK)
    zeros_state = jnp.zeros((HEAD, RWKV_WIDTH), F32)
    _, s_meta = _rwkv(zrkv_m[None], zlora_m[None], zeros_state,
                      jnp.zeros((1, RKV_COLS), F32), jnp.zeros((1, LORA_PAD), F32), rp, nc=1)

    xf = x.reshape(B * T, D)
    h1, zrkv, zlora, q, kv = _ffn_in(xf, *ffn1, row(mix_norm[0]), win, b_in, tm=FFN_ROWS)
    y_attn = _attn(q.reshape(B, T, -1), kv.reshape(B, T, -1), kv_m, attn_sinks[0].astype(F32))
    wo_rows = np.concatenate([np.arange(RWKV_WIDTH), RWKV_WIDTH + _head_cols(ATTN_HEAD_ORDER)])
    y_rwkv, _ = _rwkv(zrkv.reshape(B, T, -1), zlora.reshape(B, T, -1), s_meta[0],
                      zrkv_m[CHUNK - 1:], zlora_m[CHUNK - 1:], rp, nc=RWKV_CHUNKS_PER_STEP)
    out = _out_ffn(h1, y_rwkv.reshape(B * T, -1), y_attn.reshape(B * T, -1), w_out[0][wo_rows].astype(BF16),
                   row(ffn2_norm[0]), bf(ffn2_w_gate), bf(ffn2_w_up), bf(ffn2_w_down),
                   row(final_norm), tm=FFN_ROWS)
    return out.reshape(B, T, D)
```

```python
import functools

import jax
import jax.numpy as jnp
import numpy as np
from jax import lax
from jax.experimental import pallas as pl
from jax.experimental.pallas import tpu as pltpu

F32 = jnp.float32
BF16 = jnp.bfloat16

D_MODEL = 1024
D_FF = 2816
N_META = 16
NORM_EPS = 1e-5
CHUNK = 64

HEAD = 64
RWKV_HEADS = 8
RWKV_WIDTH = 512
LORA_W, LORA_A, LORA_G = 32, 32, 96
LORA_COLS = LORA_W + LORA_A + LORA_G
LORA_PAD = 256
GN_EPS = 64e-5

ATTN_HEADS = 8
ATTN_KV_HEADS = 2
ATTN_GROUP = 4
ATTN_WIDTH = 512
KV_WIDTH = 2 * ATTN_KV_HEADS * HEAD
WIN_CHUNKS = 2
NEG_INF = -1e30

RKV_COLS = 3 * RWKV_WIDTH
Z_COLS = RKV_COLS + LORA_PAD + ATTN_WIDTH + KV_WIDTH

RWKV_CHUNKS_PER_STEP = 8
FFN_ROWS = 512
FF_CHUNK = 256
VMEM_LIMIT = 56 * 1024 * 1024


def _const_spec(shape):
    nd = len(shape)
    return pl.BlockSpec(shape, lambda *_: (0,) * nd, pipeline_mode=pl.Buffered(1))


def _rmsnorm(x, g):
    return x * lax.rsqrt(jnp.mean(x * x, axis=-1, keepdims=True) + NORM_EPS) * g


def _swiglu(n, wg_ref, wu_ref, wd_ref):
    acc = jnp.zeros((n.shape[0], D_MODEL), F32)
    for c in range(D_FF // FF_CHUNK):
        sl = slice(c * FF_CHUNK, (c + 1) * FF_CHUNK)
        g = jnp.dot(n, wg_ref[:, sl], preferred_element_type=F32)
        u = jnp.dot(n, wu_ref[:, sl], preferred_element_type=F32)
        a = (g * jax.nn.sigmoid(g) * u).astype(BF16)
        acc = acc + jnp.dot(a, wd_ref[sl, :], preferred_element_type=F32)
    return acc


def _ffn_in_kernel(x_ref, g1_ref, wg_ref, wu_ref, wd_ref, g2_ref, win_ref, bin_ref,
                   h_ref, zrkv_ref, zlora_ref, q_ref, kv_ref):
    x = x_ref[...]
    n1 = _rmsnorm(x, g1_ref[...]).astype(BF16)
    h = x + 0.5 * _swiglu(n1, wg_ref, wu_ref, wd_ref)
    h_ref[...] = h
    n2 = _rmsnorm(h, g2_ref[...]).astype(BF16)
    z = jnp.dot(n2, win_ref[...], preferred_element_type=F32) + bin_ref[...]
    o = 0
    for ref, w in ((zrkv_ref, RKV_COLS), (zlora_ref, LORA_PAD), (q_ref, ATTN_WIDTH), (kv_ref, KV_WIDTH)):
        ref[...] = z[:, o:o + w].astype(ref.dtype)
        o += w


def _ffn_in(x, g1, wg, wu, wd, g2, win, b_in, tm):
    rows = x.shape[0]
    row = lambda w: pl.BlockSpec((tm, w), lambda i: (i, 0))
    return pl.pallas_call(
        _ffn_in_kernel,
        grid=(rows // tm,),
        in_specs=[row(D_MODEL), _const_spec((1, D_MODEL)), _const_spec((D_MODEL, D_FF)),
                  _const_spec((D_MODEL, D_FF)), _const_spec((D_FF, D_MODEL)), _const_spec((1, D_MODEL)),
                  _const_spec((D_MODEL, Z_COLS)), _const_spec((1, Z_COLS))],
        out_specs=[row(D_MODEL), row(RKV_COLS), row(LORA_PAD), row(ATTN_WIDTH), row(KV_WIDTH)],
        out_shape=[jax.ShapeDtypeStruct((rows, w), t)
                   for w, t in ((D_MODEL, F32), (RKV_COLS, F32), (LORA_PAD, F32), (ATTN_WIDTH, BF16),
                                (KV_WIDTH, BF16))],
        compiler_params=pltpu.CompilerParams(dimension_semantics=("arbitrary",),
                                             vmem_limit_bytes=VMEM_LIMIT),
        name="ffn_in",
    )(x, g1, wg, wu, wd, g2, win, b_in)


QUAD = 4 * HEAD
N_QUADS = RWKV_WIDTH // QUAD


def _blockdiag(x, bd_ref):
    return jnp.concatenate([x * bd_ref[h * HEAD:(h + 1) * HEAD, :] for h in range(4)], axis=0)


def _mm(lhs, rhs, bd_ref, nt=False):
    many = isinstance(lhs, (list, tuple))
    l = jnp.concatenate([x.astype(BF16) for x in lhs], axis=0) if many else lhs.astype(BF16)
    dims = (((1,), (1,)), ((), ())) if nt else (((1,), (0,)), ((), ()))
    out = lax.dot_general(l, _blockdiag(rhs.astype(BF16), bd_ref), dims, preferred_element_type=F32)
    return [out[i * CHUNK:(i + 1) * CHUNK] for i in range(len(lhs))] if many else out


def _mm_tn_diag(lhs_list, rhs_list, head_of_lane):
    lt = jnp.concatenate([l.T for l in lhs_list], axis=1).astype(BF16)
    r = jnp.concatenate(rhs_list, axis=0).astype(BF16)
    full = jnp.dot(lt, r, preferred_element_type=F32)
    out = full[3 * HEAD:4 * HEAD]
    for h in (2, 1, 0):
        out = jnp.where(head_of_lane == h, full[h * HEAD:(h + 1) * HEAD], out)
    return out


def _unit_lower_inverse(a_list, t_idx, j_idx, mm):
    same = lambda n: (t_idx // n) == (j_idx // n)
    eye = (t_idx == j_idx).astype(F32)
    a8 = [jnp.where(same(8), a, 0.0) for a in a_list]
    a2 = [mm(x, x) for x in a8]
    yield
    inv = [eye + x for x in a8]
    both = [mm([i, x], x) for i, x in zip(inv, a2)]
    inv = [i + t[0] for i, t in zip(inv, both)]
    yield
    inv = [i + mm(i, t[1]) for i, t in zip(inv, both)]
    yield
    for n in (16, 32, 64):
        off = same(n) & jnp.logical_not(same(n // 2))
        ci = [mm(jnp.where(off, a, 0.0), i) for a, i in zip(a_list, inv)]
        yield
        inv = [i + mm(i, x) for i, x in zip(inv, ci)]
        yield
    return inv


def _drain(gen):
    while True:
        try:
            next(gen)
        except StopIteration as stop:
            return stop.value


def _interleave(major, minor):
    minor_value, minor_done = None, False
    while True:
        try:
            next(major)
        except StopIteration as stop:
            major_value = stop.value
            break
        if not minor_done:
            try:
                next(minor)
            except StopIteration as stop:
                minor_value, minor_done = stop.value, True
    return major_value, (minor_value if minor_done else _drain(minor))


def _shift_lerp(z, prev, mu, row0):
    shifted = jnp.where(row0, prev, pltpu.roll(z, 1, axis=0))
    return z + (shifted - z) * mu


def _group_sum(x, bd_ref):
    return jnp.concatenate(
        [jnp.dot(x[:, q * QUAD:(q + 1) * QUAD].astype(BF16), bd_ref[...], preferred_element_type=F32)
         for q in range(N_QUADS)], axis=1)


def _rwkv_prep(zrkv_raw, zlora_raw, prev_rkv, prev_lora, refs):
    (mu_rkv_ref, mu_lora_ref, wlora_ref, w0_ref, a0_ref, kk_ref, ka_ref, rk_ref, bd_ref) = refs
    T = CHUNK
    R = zrkv_raw.shape[0]
    row0 = lax.broadcasted_iota(jnp.int32, (R, 1), 0) == 0
    zrkv = _shift_lerp(zrkv_raw, prev_rkv, mu_rkv_ref[...], row0)
    zlora = _shift_lerp(zlora_raw, prev_lora, mu_lora_ref[...], row0)
    zr = zrkv[:, 0:RWKV_WIDTH]
    zk = zrkv[:, RWKV_WIDTH:2 * RWKV_WIDTH]
    zv = zrkv[:, 2 * RWKV_WIDTH:3 * RWKV_WIDTH]
    yield

    lane = lax.broadcasted_iota(jnp.int32, (R, LORA_PAD), 1)
    act = jnp.where(lane < LORA_W, jnp.tanh(zlora),
                    jnp.where(lane < LORA_W + LORA_A, zlora, jax.nn.sigmoid(zlora)))
    lo = jnp.dot(act.astype(BF16), wlora_ref[...], preferred_element_type=F32)
    yield
    lw = -jnp.exp(F32(-0.5)) * jax.nn.sigmoid(w0_ref[...] + lo[:, 0:RWKV_WIDTH])
    a = jax.nn.sigmoid(a0_ref[...] + lo[:, RWKV_WIDTH:2 * RWKV_WIDTH])
    g = lo[:, 2 * RWKV_WIDTH:3 * RWKV_WIDTH]
    yield

    kkn = zk * kk_ref[...]
    kk = kkn * jnp.minimum(lax.rsqrt(_group_sum(kkn * kkn, bd_ref)), 1e12)
    yield
    k = zk * (1.0 + (a - 1.0) * ka_ref[...])
    b = kk * a
    bonus = _group_sum(zr * k * rk_ref[...], bd_ref) * zv
    yield

    rr = lax.broadcasted_iota(jnp.int32, (R, R), 0)
    cc = lax.broadcasted_iota(jnp.int32, (R, R), 1)
    tri = jnp.where(((rr // T) == (cc // T)) & (cc <= rr), 1.0, 0.0).astype(BF16)
    lw_hi = lw.astype(BF16)
    lw_lo = (lw - lw_hi.astype(F32)).astype(BF16)
    cum = jnp.dot(tri, lw_hi, preferred_element_type=F32) + jnp.dot(tri, lw_lo, preferred_element_type=F32)
    row_of_chunk = lambda r: jnp.concatenate(
        [jnp.broadcast_to(cum[n * T + r:n * T + r + 1], (T, RWKV_WIDTH)) for n in range(R // T)], axis=0)
    mid = row_of_chunk(T // 2 - 1)
    end = row_of_chunk(T - 1)
    yield
    e = jnp.exp(cum)
    first = lax.broadcasted_iota(jnp.int32, (R, 1), 0) % T == 0
    e_prev = jnp.where(first, 1.0, pltpu.roll(e, 1, axis=0))
    e_mid = jnp.exp(-mid)
    yield
    e_neg = jnp.exp(mid - cum)
    e_end = jnp.exp(end - cum)
    yield
    at0 = -kk * e_prev
    rt0 = zr * e
    ops = dict(at=at0 * e_mid, rt=rt0 * e_mid, at0=at0, rt0=rt0, v=zv, e=e, g=g, bonus=bonus)
    yield
    ops.update(bt=b * e_neg, kt=k * e_neg, b_end=b * e_end, k_end=k * e_end)
    return ops


def _rwkv_chunks(ops, s, lnw_ref, lnb_ref, bd_ref):
    T = CHUNK
    nc = ops["v"].shape[0] // T
    t_idx = lax.broadcasted_iota(jnp.int32, (T, QUAD), 0)
    l_idx = lax.broadcasted_iota(jnp.int32, (T, QUAD), 1)
    j_idx = l_idx % HEAD
    head_of_lane = l_idx // HEAD
    strict = j_idx < t_idx
    incl = j_idx <= t_idx
    mm = functools.partial(_mm, bd_ref=bd_ref)

    probs = [(n, q) for n in range(nc) for q in range(N_QUADS)]
    cut = lambda name, p: ops[name][p[0] * T:(p[0] + 1) * T, p[1] * QUAD:(p[1] + 1) * QUAD]
    xs = [jnp.concatenate([cut("at", p), cut("rt", p)], axis=0) for p in probs]
    xb = [mm(x, cut("bt", p), nt=True) for x, p in zip(xs, probs)]
    yield
    xk = [mm(x, cut("kt", p), nt=True) for x, p in zip(xs, probs)]
    a_ab = [jnp.where(strict, v[:T], 0.0) for v in xb]
    a_rb = [jnp.where(incl, v[T:], 0.0) for v in xb]
    a_ak = [jnp.where(strict, v[:T], 0.0) for v in xk]
    a_rk = [jnp.where(incl, v[T:], 0.0) for v in xk]
    yield
    w1yv = [mm([m1, m2], cut("v", p)) for m1, m2, p in zip(a_ak, a_rk, probs)]
    w1 = [t[0] for t in w1yv]
    yv = [t[1] for t in w1yv]
    yield
    inv = yield from _unit_lower_inverse(a_ab, t_idx, j_idx, mm)
    au = [mm(i, cut("at0", p)) for i, p in zip(inv, probs)]
    u0 = [mm(i, w) for i, w in zip(inv, w1)]
    yield
    r_eff = [cut("rt0", p) + mm(m, x) for m, x, p in zip(a_rb, au, probs)]
    y0 = [v + mm(m, u) for v, m, u in zip(yv, a_rb, u0)]
    yield
    m_all = [_mm_tn_diag([x], [cut("b_end", p)], head_of_lane) for x, p in zip(au, probs)]
    yield
    g_all = [_mm_tn_diag([u, cut("v", p)], [cut("b_end", p), cut("k_end", p)], head_of_lane)
             for u, p in zip(u0, probs)]
    yield

    s = list(s)
    y_rows = []
    for n in range(nc):
        ys = []
        for q in range(N_QUADS):
            i = n * N_QUADS + q
            w_tot = ops["e"][(n + 1) * T - 1:(n + 1) * T, q * QUAD:(q + 1) * QUAD]
            s_new = s[q] * w_tot + mm(s[q], m_all[i]) + g_all[i]
            ys.append(y0[i] + mm(r_eff[i], s[q], nt=True))
            s[q] = s_new
        y_rows.append(jnp.concatenate(ys, axis=1))
        yield
    y = jnp.concatenate(y_rows, axis=0)
    yc = y - _group_sum(y, bd_ref) * (1.0 / HEAD)
    var = _group_sum(yc * yc, bd_ref) * (1.0 / HEAD)
    yn = yc * lax.rsqrt(var + GN_EPS) * lnw_ref[...] + lnb_ref[...]
    return (yn + ops["bonus"]) * ops["g"], s


def _rwkv_kernel(zrkv_ref, zlora_ref, s0_ref, prev_rkv_ref, prev_lora_ref, mu_rkv_ref, mu_lora_ref,
                 wlora_ref, w0_ref, a0_ref, kk_ref, ka_ref, rk_ref, lnw_ref, lnb_ref, bd_ref,
                 y_ref, sfin_ref, s_scr, prkv_scr, plora_scr, *, nc):
    step = pl.program_id(1)

    @pl.when(step == 0)
    def _():
        s_scr[...] = s0_ref[...]
        prkv_scr[...] = prev_rkv_ref[...]
        plora_scr[...] = prev_lora_ref[...]

    R = nc * CHUNK
    prep_refs = (mu_rkv_ref, mu_lora_ref, wlora_ref, w0_ref, a0_ref, kk_ref, ka_ref, rk_ref, bd_ref)
    halves = [slice(0, R)] if nc == 1 else [slice(0, R // 2), slice(R // 2, R)]

    def prep(rows, prev_rkv, prev_lora):
        return _rwkv_prep(zrkv_ref[0, rows, :], zlora_ref[0, rows, :], prev_rkv, prev_lora, prep_refs)

    last_row = lambda ref, rows: ref[0, rows.stop - 1:rows.stop, :]
    s = [s_scr[:, q * QUAD:(q + 1) * QUAD] for q in range(N_QUADS)]
    ops = _drain(prep(halves[0], prkv_scr[...], plora_scr[...]))
    for i, rows in enumerate(halves):
        chunks = _rwkv_chunks(ops, s, lnw_ref, lnb_ref, bd_ref)
        if i + 1 < len(halves):
            nxt = prep(halves[i + 1], last_row(zrkv_ref, rows), last_row(zlora_ref, rows))
            (y, s), ops = _interleave(chunks, nxt)
        else:
            y, s = _drain(chunks)
        y_ref[0, rows, :] = y.astype(y_ref.dtype)
    prkv_scr[...] = last_row(zrkv_ref, halves[-1])
    plora_scr[...] = last_row(zlora_ref, halves[-1])
    for q in range(N_QUADS):
        s_scr[:, q * QUAD:(q + 1) * QUAD] = s[q]

    @pl.when(step == pl.num_programs(1) - 1)
    def _():
        sfin_ref[0] = s_scr[...]


def _rwkv(zrkv, zlora, s0, prev_rkv, prev_lora, p, nc):
    B, T, _ = zrkv.shape
    rows = nc * CHUNK
    blk = lambda w: pl.BlockSpec((1, rows, w), lambda b, c: (b, c, 0))
    vec = lambda w: _const_spec((1, w))
    state = (HEAD, RWKV_WIDTH)
    return pl.pallas_call(
        functools.partial(_rwkv_kernel, nc=nc),
        grid=(B, T // rows),
        in_specs=[blk(RKV_COLS), blk(LORA_PAD), _const_spec(state), vec(RKV_COLS), vec(LORA_PAD),
                  vec(RKV_COLS), vec(LORA_PAD), _const_spec((LORA_PAD, 3 * RWKV_WIDTH))]
                 + [vec(RWKV_WIDTH)] * 7 + [_const_spec((QUAD, QUAD))],
        out_specs=[blk(RWKV_WIDTH), pl.BlockSpec((1,) + state, lambda b, c: (b, 0, 0))],
        out_shape=[jax.ShapeDtypeStruct((B, T, RWKV_WIDTH), BF16),
                   jax.ShapeDtypeStruct((B,) + state, F32)],
        scratch_shapes=[pltpu.VMEM(state, F32), pltpu.VMEM((1, RKV_COLS), F32),
                        pltpu.VMEM((1, LORA_PAD), F32)],
        compiler_params=pltpu.CompilerParams(dimension_semantics=("arbitrary", "arbitrary"),
                                             vmem_limit_bytes=VMEM_LIMIT),
        name="rwkv",
    )(zrkv, zlora, s0, prev_rkv, prev_lora, p["mu_rkv"], p["mu_lora"], p["wlora"], p["w0"], p["a0"],
      p["k_k"], p["k_a"], p["r_k"], p["ln_w"], p["ln_b"], p["bd"])


N_WIN = (WIN_CHUNKS + 1) * CHUNK
KEY_SLOTS = 256
ATTN_CHUNKS_PER_STEP = 8
PAIR = 2 * HEAD
ATTN_HEAD_ORDER = tuple(h for p in range(ATTN_GROUP) for h in (p, ATTN_GROUP + p))


def _attn_bias():
    p = np.arange(ATTN_GROUP)[:, None, None, None]
    i = np.arange(CHUNK)[None, :, None, None]
    j = np.arange(ATTN_KV_HEADS)[None, None, :, None]
    s = np.arange(KEY_SLOTS)[None, None, None, :]
    slope = np.exp2(-8.0 * (ATTN_GROUP * j + p + 1.0) / ATTN_HEADS)
    meta = s >= KEY_SLOTS - N_META
    dist = np.where(meta, N_META + i - (s - (KEY_SLOTS - N_META)), np.abs(WIN_CHUNKS * CHUNK + i - s))
    b0 = np.where((s >= N_WIN) & ~meta, NEG_INF, -slope * dist)
    b1 = np.broadcast_to(np.where(meta, slope * CHUNK, 0.0), b0.shape)
    shape = (ATTN_GROUP * CHUNK, ATTN_KV_HEADS * KEY_SLOTS)
    return (jnp.asarray(b0.reshape(shape), F32), jnp.asarray(b1.reshape(shape), F32))


def _attn_kernel(sink_ref, q_ref, kvp_ref, kvc_ref, kvm_ref, b0_ref, b1_ref, o_ref):
    step = pl.program_id(1)
    nchunks = ATTN_CHUNKS_PER_STEP
    rows = ATTN_GROUP * CHUNK
    kv_rows = jnp.concatenate([kvp_ref[0], kvc_ref[0]], axis=0)
    kvm = kvm_ref[...]
    first_head = lax.broadcasted_iota(jnp.int32, (1, PAIR), 1) < HEAD
    slot = lax.broadcasted_iota(jnp.int32, (1, ATTN_KV_HEADS * KEY_SLOTS), 1) % KEY_SLOTS
    pair_of_row = lax.broadcasted_iota(jnp.int32, (rows, 1), 0) // CHUNK
    sinks = []
    for j in range(ATTN_KV_HEADS):
        col = jnp.zeros((rows, 1), F32)
        for p in range(ATTN_GROUP):
            col = jnp.where(pair_of_row == p, sink_ref[ATTN_GROUP * j + p], col)
        sinks.append(col)

    def per_head(x):
        zero = jnp.zeros_like(x)
        return jnp.concatenate([jnp.where(first_head, x, zero), jnp.where(first_head, zero, x)], axis=0)

    chunk_ids = [step * nchunks + n for n in range(nchunks)]
    keys = [jnp.concatenate([kv_rows[n * CHUNK:n * CHUNK + N_WIN], kvm], axis=0) for n in range(nchunks)]
    qs = [jnp.concatenate([q_ref[0, n * CHUNK:(n + 1) * CHUNK, p * PAIR:(p + 1) * PAIR]
                           for p in range(ATTN_GROUP)], axis=0) * (HEAD ** -0.5) for n in range(nchunks)]
    s = [lax.dot_general(q, per_head(kx[:, :PAIR]), (((1,), (1,)), ((), ())), preferred_element_type=F32)
         for q, kx in zip(qs, keys)]
    s = [x + b0_ref[...] - c.astype(F32) * b1_ref[...] for x, c in zip(s, chunk_ids)]
    s = [jnp.where(slot >= (WIN_CHUNKS - c) * CHUNK, x, NEG_INF) for x, c in zip(s, chunk_ids)]
    outs = []
    for n in range(nchunks):
        ps, inv_den = [], []
        for j in range(ATTN_KV_HEADS):
            sj = s[n][:, j * KEY_SLOTS:(j + 1) * KEY_SLOTS]
            m = jnp.maximum(jnp.max(sj, axis=-1, keepdims=True), sinks[j])
            pj = jnp.exp(sj - m)
            inv_den.append(1.0 / (jnp.sum(pj, axis=-1, keepdims=True) + jnp.exp(sinks[j] - m)))
            ps.append(pj.astype(BF16))
        o = jnp.dot(jnp.concatenate(ps, axis=1), per_head(keys[n][:, PAIR:]),
                    preferred_element_type=F32)
        o = o * jnp.where(first_head, inv_den[0], inv_den[1])
        outs.append(jnp.concatenate([o[p * CHUNK:(p + 1) * CHUNK] for p in range(ATTN_GROUP)], axis=1))
    o_ref[0] = jnp.concatenate(outs, axis=0).astype(o_ref.dtype)


def _attn(q, kv, kv_meta, sinks):
    B, T, _ = q.shape
    rows = ATTN_CHUNKS_PER_STEP * CHUNK
    back = WIN_CHUNKS * CHUNK
    b0, b1 = _attn_bias()
    return pl.pallas_call(
        _attn_kernel,
        grid=(B, T // rows),
        in_specs=[pl.BlockSpec(memory_space=pltpu.SMEM),
                  pl.BlockSpec((1, rows, ATTN_WIDTH), lambda b, c: (b, c, 0)),
                  pl.BlockSpec((1, back, KV_WIDTH),
                               lambda b, c: (b, jnp.maximum(c * (rows // back) - 1, 0), 0)),
                  pl.BlockSpec((1, rows, KV_WIDTH), lambda b, c: (b, c, 0)),
                  _const_spec((CHUNK, KV_WIDTH)), _const_spec(b0.shape), _const_spec(b1.shape)],
        out_specs=pl.BlockSpec((1, rows, ATTN_WIDTH), lambda b, c: (b, c, 0)),
        out_shape=jax.ShapeDtypeStruct((B, T, ATTN_WIDTH), BF16),
        compiler_params=pltpu.CompilerParams(dimension_semantics=("arbitrary", "arbitrary"),
                                             vmem_limit_bytes=VMEM_LIMIT),
        name="attn",
    )(sinks, q, kv, kv, kv_meta, b0, b1)


def _out_ffn_kernel(h_ref, yr_ref, ya_ref, wo_ref, g_ref, wg_ref, wu_ref, wd_ref, gf_ref, o_ref):
    y = jnp.concatenate([yr_ref[...], ya_ref[...]], axis=-1)
    h = h_ref[...] + jnp.dot(y, wo_ref[...], preferred_element_type=F32)
    n = _rmsnorm(h, g_ref[...]).astype(BF16)
    h = h + 0.5 * _swiglu(n, wg_ref, wu_ref, wd_ref)
    o_ref[...] = _rmsnorm(h, gf_ref[...])


def _out_ffn(h, y_rwkv, y_attn, wo, g, wg, wu, wd, gf, tm):
    rows = h.shape[0]
    row = lambda w: pl.BlockSpec((tm, w), lambda i: (i, 0))
    return pl.pallas_call(
        _out_ffn_kernel,
        grid=(rows // tm,),
        in_specs=[row(D_MODEL), row(RWKV_WIDTH), row(ATTN_WIDTH), _const_spec((D_MODEL, D_MODEL)),
                  _const_spec((1, D_MODEL)), _const_spec((D_MODEL, D_FF)), _const_spec((D_MODEL, D_FF)),
                  _const_spec((D_FF, D_MODEL)), _const_spec((1, D_MODEL))],
        out_specs=row(D_MODEL),
        out_shape=jax.ShapeDtypeStruct((rows, D_MODEL), F32),
        compiler_params=pltpu.CompilerParams(dimension_semantics=("arbitrary",),
                                             vmem_limit_bytes=VMEM_LIMIT),
        name="out_ffn",
    )(h, y_rwkv, y_attn, wo, g, wg, wu, wd, gf)


def _head_cols(order):
    return np.concatenate([np.arange(h * HEAD, (h + 1) * HEAD) for h in order])


def _pack_w_in(w_in, b_attn):
    rwkv_cols = RKV_COLS + LORA_COLS
    pad = jnp.zeros((D_MODEL, LORA_PAD - LORA_COLS), w_in.dtype)
    attn_cols = np.concatenate([_head_cols(ATTN_HEAD_ORDER), np.arange(ATTN_WIDTH, ATTN_WIDTH + KV_WIDTH)])
    w = jnp.concatenate([w_in[:, :rwkv_cols], pad, w_in[:, rwkv_cols:][:, attn_cols]], axis=1).astype(BF16)
    b = jnp.concatenate([jnp.zeros((RKV_COLS + LORA_PAD,), F32), b_attn.astype(F32)[attn_cols]])[None]
    return w, b


def _pack_lora(w2, a2, g2):
    w = jnp.zeros((LORA_PAD, 3 * RWKV_WIDTH), F32)
    w = w.at[0:LORA_W, 0:RWKV_WIDTH].set(w2)
    w = w.at[LORA_W:LORA_W + LORA_A, RWKV_WIDTH:2 * RWKV_WIDTH].set(a2)
    w = w.at[LORA_W + LORA_A:LORA_COLS, 2 * RWKV_WIDTH:].set(g2)
    return w


def _block_ones():
    i = jnp.arange(QUAD) // HEAD
    return (i[:, None] == i[None, :]).astype(BF16)


def kernel(x, meta_tokens, ffn1_norm, ffn1_w_gate, ffn1_w_up, ffn1_w_down, mix_norm, w_in, b_attn, rwkv_mu, rwkv_w0, rwkv_w2, rwkv_a0, rwkv_a2, rwkv_g2, rwkv_k_k, rwkv_k_a, rwkv_r_k, rwkv_ln_w, rwkv_ln_b, attn_sinks, w_out, ffn2_norm, ffn2_w_gate, ffn2_w_up, ffn2_w_down, final_norm):
    assert ffn1_norm.shape[0] == 1, "single-layer trunk"
    B, T, D = x.shape
    row = lambda v: v.reshape(1, -1).astype(F32)
    bf = lambda w: w[0].astype(BF16)

    win, b_in = _pack_w_in(w_in[0], b_attn[0])
    mu = rwkv_mu[0].astype(F32)
    mu_lora = jnp.concatenate([mu[RKV_COLS:], jnp.zeros((LORA_PAD - LORA_COLS,), F32)])
    rp = dict(mu_rkv=row(mu[:RKV_COLS]), mu_lora=row(mu_lora),
              wlora=_pack_lora(rwkv_w2[0], rwkv_a2[0], rwkv_g2[0]).astype(BF16), bd=_block_ones(),
              w0=row(rwkv_w0[0]), a0=row(rwkv_a0[0]), k_k=row(rwkv_k_k[0]), k_a=row(rwkv_k_a[0]),
              r_k=row(rwkv_r_k[0]), ln_w=row(rwkv_ln_w[0]), ln_b=row(rwkv_ln_b[0]))
    ffn1 = (row(ffn1_norm[0]), bf(ffn1_w_gate), bf(ffn1_w_up), bf(ffn1_w_down))

    xm = jnp.concatenate([jnp.zeros((CHUNK - N_META, D), F32), meta_tokens.astype(F32)], axis=0)
    _, zrkv_m, zlora_m, _, kv_m = _ffn_in(xm, *ffn1, row(mix_norm[0]), win, b_in, tm=CHUNK)
    zeros_state = jnp.zeros((HEAD, RWKV_WIDTH), F32)
    _, s_meta = _rwkv(zrkv_m[None], zlora_m[None], zeros_state,
                      jnp.zeros((1, RKV_COLS), F32), jnp.zeros((1, LORA_PAD), F32), rp, nc=1)

    xf = x.reshape(B * T, D)
    h1, zrkv, zlora, q, kv = _ffn_in(xf, *ffn1, row(mix_norm[0]), win, b_in, tm=FFN_ROWS)
    y_rwkv, _ = _rwkv(zrkv.reshape(B, T, -1), zlora.reshape(B, T, -1), s_meta[0],
                      zrkv_m[CHUNK - 1:], zlora_m[CHUNK - 1:], rp, nc=RWKV_CHUNKS_PER_STEP)
    y_attn = _attn(q.reshape(B, T, -1), kv.reshape(B, T, -1), kv_m, attn_sinks[0].astype(F32))
    wo_rows = np.concatenate([np.arange(RWKV_WIDTH), RWKV_WIDTH + _head_cols(ATTN_HEAD_ORDER)])
    out = _out_ffn(h1, y_rwkv.reshape(B * T, -1), y_attn.reshape(B * T, -1), w_out[0][wo_rows].astype(BF16),
                   row(ffn2_norm[0]), bf(ffn2_w_gate), bf(ffn2_w_up), bf(ffn2_w_down),
                   row(final_norm), tm=FFN_ROWS)
    return out.reshape(B, T, D)
```

```python
import functools

import jax
import jax.numpy as jnp
import numpy as np
from jax import lax
from jax.experimental import pallas as pl
from jax.experimental.pallas import tpu as pltpu

F32 = jnp.float32
BF16 = jnp.bfloat16

D_MODEL = 1024
D_FF = 2816
N_META = 16
NORM_EPS = 1e-5
CHUNK = 64

HEAD = 64
RWKV_HEADS = 8
RWKV_WIDTH = 512
LORA_W, LORA_A, LORA_G = 32, 32, 96
LORA_COLS = LORA_W + LORA_A + LORA_G
LORA_PAD = 256
GN_EPS = 64e-5

ATTN_HEADS = 8
ATTN_KV_HEADS = 2
ATTN_GROUP = 4
ATTN_WIDTH = 512
KV_WIDTH = 2 * ATTN_KV_HEADS * HEAD
WIN_CHUNKS = 2
NEG_INF = -1e30

RKV_COLS = 3 * RWKV_WIDTH
Z_COLS = RKV_COLS + LORA_PAD + ATTN_WIDTH + KV_WIDTH

RWKV_CHUNKS_PER_STEP = 8
FFN_ROWS = 512
FF_CHUNK = 256
VMEM_LIMIT = 56 * 1024 * 1024


def _const_spec(shape):
    nd = len(shape)
    return pl.BlockSpec(shape, lambda *_: (0,) * nd, pipeline_mode=pl.Buffered(1))


def _rmsnorm(x, g):
    return x * lax.rsqrt(jnp.mean(x * x, axis=-1, keepdims=True) + NORM_EPS) * g


def _swiglu(n, wg_ref, wu_ref, wd_ref):
    acc = jnp.zeros((n.shape[0], D_MODEL), F32)
    for c in range(D_FF // FF_CHUNK):
        sl = slice(c * FF_CHUNK, (c + 1) * FF_CHUNK)
        g = jnp.dot(n, wg_ref[:, sl], preferred_element_type=F32)
        u = jnp.dot(n, wu_ref[:, sl], preferred_element_type=F32)
        a = (g * jax.nn.sigmoid(g) * u).astype(BF16)
        acc = acc + jnp.dot(a, wd_ref[sl, :], preferred_element_type=F32)
    return acc


def _ffn_in_kernel(x_ref, g1_ref, wg_ref, wu_ref, wd_ref, g2_ref, win_ref, bin_ref,
                   h_ref, zrkv_ref, zlora_ref, q_ref, kv_ref):
    x = x_ref[...]
    n1 = _rmsnorm(x, g1_ref[...]).astype(BF16)
    h = x + 0.5 * _swiglu(n1, wg_ref, wu_ref, wd_ref)
    h_ref[...] = h
    n2 = _rmsnorm(h, g2_ref[...]).astype(BF16)
    z = jnp.dot(n2, win_ref[...], preferred_element_type=F32) + bin_ref[...]
    o = 0
    for ref, w in ((zrkv_ref, RKV_COLS), (zlora_ref, LORA_PAD), (q_ref, ATTN_WIDTH), (kv_ref, KV_WIDTH)):
        ref[...] = z[:, o:o + w].astype(ref.dtype)
        o += w


def _ffn_in(x, g1, wg, wu, wd, g2, win, b_in, tm):
    rows = x.shape[0]
    row = lambda w: pl.BlockSpec((tm, w), lambda i: (i, 0))
    return pl.pallas_call(
        _ffn_in_kernel,
        grid=(rows // tm,),
        in_specs=[row(D_MODEL), _const_spec((1, D_MODEL)), _const_spec((D_MODEL, D_FF)),
                  _const_spec((D_MODEL, D_FF)), _const_spec((D_FF, D_MODEL)), _const_spec((1, D_MODEL)),
                  _const_spec((D_MODEL, Z_COLS)), _const_spec((1, Z_COLS))],
        out_specs=[row(D_MODEL), row(RKV_COLS), row(LORA_PAD), row(ATTN_WIDTH), row(KV_WIDTH)],
        out_shape=[jax.ShapeDtypeStruct((rows, w), t)
                   for w, t in ((D_MODEL, F32), (RKV_COLS, F32), (LORA_PAD, F32), (ATTN_WIDTH, BF16),
                                (KV_WIDTH, BF16))],
        compiler_params=pltpu.CompilerParams(dimension_semantics=("arbitrary",),
                                             vmem_limit_bytes=VMEM_LIMIT),
        name="ffn_in",
    )(x, g1, wg, wu, wd, g2, win, b_in)


QUAD = 4 * HEAD
N_QUADS = RWKV_WIDTH // QUAD


def _blockdiag(x, bd_ref):
    return jnp.concatenate([x * bd_ref[h * HEAD:(h + 1) * HEAD, :] for h in range(4)], axis=0)


def _mm(lhs, rhs, bd_ref, nt=False):
    many = isinstance(lhs, (list, tuple))
    l = jnp.concatenate([x.astype(BF16) for x in lhs], axis=0) if many else lhs.astype(BF16)
    dims = (((1,), (1,)), ((), ())) if nt else (((1,), (0,)), ((), ()))
    out = lax.dot_general(l, _blockdiag(rhs.astype(BF16), bd_ref), dims, preferred_element_type=F32)
    return [out[i * CHUNK:(i + 1) * CHUNK] for i in range(len(lhs))] if many else out


def _mm_tn_diag(lhs_list, rhs_list, head_of_lane):
    lt = jnp.concatenate([l.T for l in lhs_list], axis=1).astype(BF16)
    r = jnp.concatenate(rhs_list, axis=0).astype(BF16)
    full = jnp.dot(lt, r, preferred_element_type=F32)
    out = full[3 * HEAD:4 * HEAD]
    for h in (2, 1, 0):
        out = jnp.where(head_of_lane == h, full[h * HEAD:(h + 1) * HEAD], out)
    return out


def _unit_lower_inverse(a_list, t_idx, j_idx, mm):
    same = lambda n: (t_idx // n) == (j_idx // n)
    eye = (t_idx == j_idx).astype(F32)
    a8 = [jnp.where(same(8), a, 0.0) for a in a_list]
    a2 = [mm(x, x) for x in a8]
    inv = [eye + x for x in a8]
    both = [mm([i, x], x) for i, x in zip(inv, a2)]
    inv = [i + t[0] for i, t in zip(inv, both)]
    inv = [i + mm(i, t[1]) for i, t in zip(inv, both)]
    for n in (16, 32, 64):
        off = same(n) & jnp.logical_not(same(n // 2))
        ci = [mm(jnp.where(off, a, 0.0), i) for a, i in zip(a_list, inv)]
        inv = [i + mm(i, x) for i, x in zip(inv, ci)]
    return inv


def _shift_lerp(z, prev, mu, row0):
    shifted = jnp.where(row0, prev, pltpu.roll(z, 1, axis=0))
    return z + (shifted - z) * mu


def _group_sum(x, bd_ref):
    return jnp.concatenate(
        [jnp.dot(x[:, q * QUAD:(q + 1) * QUAD].astype(BF16), bd_ref[...], preferred_element_type=F32)
         for q in range(N_QUADS)], axis=1)


def _rwkv_kernel(zrkv_ref, zlora_ref, s0_ref, prev_rkv_ref, prev_lora_ref, mu_rkv_ref, mu_lora_ref,
                 wlora_ref, w0_ref, a0_ref, kk_ref, ka_ref, rk_ref, lnw_ref, lnb_ref, bd_ref, tri_ref,
                 y_ref, sfin_ref, s_scr, prkv_scr, plora_scr, *, nc):
    step = pl.program_id(1)

    @pl.when(step == 0)
    def _():
        s_scr[...] = s0_ref[...]
        prkv_scr[...] = prev_rkv_ref[CHUNK - 1:CHUNK, :]
        plora_scr[...] = prev_lora_ref[CHUNK - 1:CHUNK, :]

    T = CHUNK
    R = nc * T
    zrkv_raw = zrkv_ref[0]
    zlora_raw = zlora_ref[0]
    row0 = lax.broadcasted_iota(jnp.int32, (R, 1), 0) == 0
    zrkv = _shift_lerp(zrkv_raw, prkv_scr[...], mu_rkv_ref[...], row0)
    zlora = _shift_lerp(zlora_raw, plora_scr[...], mu_lora_ref[...], row0)
    prkv_scr[...] = zrkv_raw[R - 1:R, :]
    plora_scr[...] = zlora_raw[R - 1:R, :]

    zr = zrkv[:, 0:RWKV_WIDTH]
    zk = zrkv[:, RWKV_WIDTH:2 * RWKV_WIDTH]
    zv = zrkv[:, 2 * RWKV_WIDTH:3 * RWKV_WIDTH]

    lane = lax.broadcasted_iota(jnp.int32, (R, LORA_PAD), 1)
    act = jnp.where(lane < LORA_W, jnp.tanh(zlora),
                    jnp.where(lane < LORA_W + LORA_A, zlora, jax.nn.sigmoid(zlora)))
    lo = jnp.dot(act.astype(BF16), wlora_ref[...], preferred_element_type=F32)
    lw = -jnp.exp(F32(-0.5)) * jax.nn.sigmoid(w0_ref[...] + lo[:, 0:RWKV_WIDTH])
    a = jax.nn.sigmoid(a0_ref[...] + lo[:, RWKV_WIDTH:2 * RWKV_WIDTH])
    g = lo[:, 2 * RWKV_WIDTH:3 * RWKV_WIDTH]

    kkn = zk * kk_ref[...]
    kk = kkn * jnp.minimum(lax.rsqrt(_group_sum(kkn * kkn, bd_ref)), 1e12)
    k = zk * (1.0 + (a - 1.0) * ka_ref[...])
    b = kk * a
    bonus = _group_sum(zr * k * rk_ref[...], bd_ref) * zv

    t_idx = lax.broadcasted_iota(jnp.int32, (T, QUAD), 0)
    l_idx = lax.broadcasted_iota(jnp.int32, (T, QUAD), 1)
    j_idx = l_idx % HEAD
    head_of_lane = l_idx // HEAD
    strict = j_idx < t_idx
    incl = j_idx <= t_idx
    mm = functools.partial(_mm, bd_ref=bd_ref)

    tri = tri_ref[...]
    lw_hi = lw.astype(BF16)
    lw_lo = (lw - lw_hi.astype(F32)).astype(BF16)
    cum = jnp.dot(tri, lw_hi, preferred_element_type=F32) + jnp.dot(tri, lw_lo, preferred_element_type=F32)
    row_of_chunk = lambda r: jnp.concatenate(
        [jnp.broadcast_to(cum[n * T + r:n * T + r + 1], (T, RWKV_WIDTH)) for n in range(nc)], axis=0)
    mid = row_of_chunk(T // 2 - 1)
    end = row_of_chunk(T - 1)
    e = jnp.exp(cum)
    first = lax.broadcasted_iota(jnp.int32, (R, 1), 0) % T == 0
    e_prev = jnp.where(first, 1.0, pltpu.roll(e, 1, axis=0))
    e_mid = jnp.exp(-mid)
    e_neg = jnp.exp(mid - cum)
    e_end = jnp.exp(end - cum)
    at0 = -kk * e_prev
    rt0 = zr * e
    at = at0 * e_mid
    rt = rt0 * e_mid
    bt = b * e_neg
    kt = k * e_neg
    b_end = b * e_end
    k_end = k * e_end

    probs = [(n, q) for n in range(nc) for q in range(N_QUADS)]
    cut = lambda arr, p: arr[p[0] * T:(p[0] + 1) * T, p[1] * QUAD:(p[1] + 1) * QUAD]
    xs = [jnp.concatenate([cut(at, p), cut(rt, p)], axis=0) for p in probs]
    xb = [mm(x, cut(bt, p), nt=True) for x, p in zip(xs, probs)]
    xk = [mm(x, cut(kt, p), nt=True) for x, p in zip(xs, probs)]
    a_ab = [jnp.where(strict, v[:T], 0.0) for v in xb]
    a_rb = [jnp.where(incl, v[T:], 0.0) for v in xb]
    a_ak = [jnp.where(strict, v[:T], 0.0) for v in xk]
    a_rk = [jnp.where(incl, v[T:], 0.0) for v in xk]
    w1yv = [mm([m1, m2], cut(zv, p)) for m1, m2, p in zip(a_ak, a_rk, probs)]
    w1 = [t[0] for t in w1yv]
    yv = [t[1] for t in w1yv]
    inv = _unit_lower_inverse(a_ab, t_idx, j_idx, mm)
    au = [mm(i, cut(at0, p)) for i, p in zip(inv, probs)]
    u0 = [mm(i, w) for i, w in zip(inv, w1)]
    r_eff = [cut(rt0, p) + mm(m, x) for m, x, p in zip(a_rb, au, probs)]
    y0 = [v + mm(m, u) for v, m, u in zip(yv, a_rb, u0)]
    m_all = [_mm_tn_diag([x], [cut(b_end, p)], head_of_lane) for x, p in zip(au, probs)]
    g_all = [_mm_tn_diag([u, cut(zv, p)], [cut(b_end, p), cut(k_end, p)], head_of_lane)
             for u, p in zip(u0, probs)]

    s = [s_scr[:, q * QUAD:(q + 1) * QUAD] for q in range(N_QUADS)]
    y_rows = []
    for n in range(nc):
        ys = []
        for q in range(N_QUADS):
            i = n * N_QUADS + q
            w_tot = e[(n + 1) * T - 1:(n + 1) * T, q * QUAD:(q + 1) * QUAD]
            s_new = s[q] * w_tot + mm(s[q], m_all[i]) + g_all[i]
            ys.append(y0[i] + mm(r_eff[i], s[q], nt=True))
            s[q] = s_new
        y_rows.append(jnp.concatenate(ys, axis=1))
    for q in range(N_QUADS):
        s_scr[:, q * QUAD:(q + 1) * QUAD] = s[q]
    y = jnp.concatenate(y_rows, axis=0)
    yc = y - _group_sum(y, bd_ref) * (1.0 / HEAD)
    var = _group_sum(yc * yc, bd_ref) * (1.0 / HEAD)
    yn = yc * lax.rsqrt(var + GN_EPS) * lnw_ref[...] + lnb_ref[...]
    y_ref[0] = ((yn + bonus) * g).astype(y_ref.dtype)

    @pl.when(step == pl.num_programs(1) - 1)
    def _():
        sfin_ref[0] = s_scr[...]


def _rwkv(zrkv, zlora, s0, prev_rkv, prev_lora, p, nc):
    B, T, _ = zrkv.shape
    rows = nc * CHUNK
    blk = lambda w: pl.BlockSpec((1, rows, w), lambda b, c: (b, c, 0))
    vec = lambda w: _const_spec((1, w))
    state = (HEAD, RWKV_WIDTH)
    r = np.arange(rows)
    tri = jnp.asarray((r[:, None] // CHUNK == r[None, :] // CHUNK) & (r[None, :] <= r[:, None]), BF16)
    return pl.pallas_call(
        functools.partial(_rwkv_kernel, nc=nc),
        grid=(B, T // rows),
        in_specs=[blk(RKV_COLS), blk(LORA_PAD), _const_spec(state), _const_spec((CHUNK, RKV_COLS)),
                  _const_spec((CHUNK, LORA_PAD)),
                  vec(RKV_COLS), vec(LORA_PAD), _const_spec((LORA_PAD, 3 * RWKV_WIDTH))]
                 + [vec(RWKV_WIDTH)] * 7 + [_const_spec((QUAD, QUAD)), _const_spec((rows, rows))],
        out_specs=[blk(RWKV_WIDTH), pl.BlockSpec((1,) + state, lambda b, c: (b, 0, 0))],
        out_shape=[jax.ShapeDtypeStruct((B, T, RWKV_WIDTH), BF16),
                   jax.ShapeDtypeStruct((B,) + state, F32)],
        scratch_shapes=[pltpu.VMEM(state, F32), pltpu.VMEM((1, RKV_COLS), F32),
                        pltpu.VMEM((1, LORA_PAD), F32)],
        compiler_params=pltpu.CompilerParams(dimension_semantics=("arbitrary", "arbitrary"),
                                             vmem_limit_bytes=VMEM_LIMIT),
        name="rwkv",
    )(zrkv, zlora, s0, prev_rkv, prev_lora, p["mu_rkv"], p["mu_lora"], p["wlora"], p["w0"], p["a0"],
      p["k_k"], p["k_a"], p["r_k"], p["ln_w"], p["ln_b"], p["bd"], tri)


N_WIN = (WIN_CHUNKS + 1) * CHUNK
KEY_SLOTS = 256
ATTN_CHUNKS_PER_STEP = 8
PAIR = 2 * HEAD
ATTN_HEAD_ORDER = tuple(h for p in range(ATTN_GROUP) for h in (p, ATTN_GROUP + p))


def _attn_bias():
    p = np.arange(ATTN_GROUP)[:, None, None, None]
    i = np.arange(CHUNK)[None, :, None, None]
    j = np.arange(ATTN_KV_HEADS)[None, None, :, None]
    s = np.arange(KEY_SLOTS)[None, None, None, :]
    slope = np.exp2(-8.0 * (ATTN_GROUP * j + p + 1.0) / ATTN_HEADS)
    meta = s >= KEY_SLOTS - N_META
    dist = np.where(meta, N_META + i - (s - (KEY_SLOTS - N_META)), np.abs(WIN_CHUNKS * CHUNK + i - s))
    b0 = np.where((s >= N_WIN) & ~meta, NEG_INF, -slope * dist)
    b1 = np.broadcast_to(np.where(meta, slope * CHUNK, 0.0), b0.shape)
    shape = (ATTN_GROUP * CHUNK, ATTN_KV_HEADS * KEY_SLOTS)
    return (jnp.asarray(b0.reshape(shape), F32), jnp.asarray(b1.reshape(shape), F32))


def _attn_kernel(sink_ref, q_ref, kvp_ref, kvc_ref, kvm_ref, b0_ref, b1_ref, o_ref):
    step = pl.program_id(1)
    nchunks = ATTN_CHUNKS_PER_STEP
    rows = ATTN_GROUP * CHUNK
    kv_rows = jnp.concatenate([kvp_ref[0], kvc_ref[0]], axis=0)
    kvm = kvm_ref[...]
    first_head = lax.broadcasted_iota(jnp.int32, (1, PAIR), 1) < HEAD
    slot = lax.broadcasted_iota(jnp.int32, (1, ATTN_KV_HEADS * KEY_SLOTS), 1) % KEY_SLOTS
    pair_of_row = lax.broadcasted_iota(jnp.int32, (rows, 1), 0) // CHUNK
    sinks = []
    for j in range(ATTN_KV_HEADS):
        col = jnp.zeros((rows, 1), F32)
        for p in range(ATTN_GROUP):
            col = jnp.where(pair_of_row == p, sink_ref[ATTN_GROUP * j + p], col)
        sinks.append(col)

    def per_head(x):
        zero = jnp.zeros_like(x)
        return jnp.concatenate([jnp.where(first_head, x, zero), jnp.where(first_head, zero, x)], axis=0)

    chunk_ids = [step * nchunks + n for n in range(nchunks)]
    keys = [jnp.concatenate([kv_rows[n * CHUNK:n * CHUNK + N_WIN], kvm], axis=0) for n in range(nchunks)]
    qs = [jnp.concatenate([q_ref[0, n * CHUNK:(n + 1) * CHUNK, p * PAIR:(p + 1) * PAIR]
                           for p in range(ATTN_GROUP)], axis=0) * (HEAD ** -0.5) for n in range(nchunks)]
    s = [lax.dot_general(q, per_head(kx[:, :PAIR]), (((1,), (1,)), ((), ())), preferred_element_type=F32)
         for q, kx in zip(qs, keys)]
    s = [x + b0_ref[...] - c.astype(F32) * b1_ref[...] for x, c in zip(s, chunk_ids)]
    s = [jnp.where(slot >= (WIN_CHUNKS - c) * CHUNK, x, NEG_INF) for x, c in zip(s, chunk_ids)]
    outs = []
    for n in range(nchunks):
        ps, inv_den = [], []
        for j in range(ATTN_KV_HEADS):
            sj = s[n][:, j * KEY_SLOTS:(j + 1) * KEY_SLOTS]
            m = jnp.maximum(jnp.max(sj, axis=-1, keepdims=True), sinks[j])
            pj = jnp.exp(sj - m)
            inv_den.append(1.0 / (jnp.sum(pj, axis=-1, keepdims=True) + jnp.exp(sinks[j] - m)))
            ps.append(pj.astype(BF16))
        o = jnp.dot(jnp.concatenate(ps, axis=1), per_head(keys[n][:, PAIR:]),
                    preferred_element_type=F32)
        o = o * jnp.where(first_head, inv_den[0], inv_den[1])
        outs.append(jnp.concatenate([o[p * CHUNK:(p + 1) * CHUNK] for p in range(ATTN_GROUP)], axis=1))
    o_ref[0] = jnp.concatenate(outs, axis=0).astype(o_ref.dtype)


def _attn(q, kv, kv_meta, sinks):
    B, T, _ = q.shape
    rows = ATTN_CHUNKS_PER_STEP * CHUNK
    back = WIN_CHUNKS * CHUNK
    b0, b1 = _attn_bias()
    return pl.pallas_call(
        _attn_kernel,
        grid=(B, T // rows),
        in_specs=[pl.BlockSpec(memory_space=pltpu.SMEM),
                  pl.BlockSpec((1, rows, ATTN_WIDTH), lambda b, c: (b, c, 0)),
                  pl.BlockSpec((1, back, KV_WIDTH),
                               lambda b, c: (b, jnp.maximum(c * (rows // back) - 1, 0), 0)),
                  pl.BlockSpec((1, rows, KV_WIDTH), lambda b, c: (b, c, 0)),
                  _const_spec((CHUNK, KV_WIDTH)), _const_spec(b0.shape), _const_spec(b1.shape)],
        out_specs=pl.BlockSpec((1, rows, ATTN_WIDTH), lambda b, c: (b, c, 0)),
        out_shape=jax.ShapeDtypeStruct((B, T, ATTN_WIDTH), BF16),
        compiler_params=pltpu.CompilerParams(dimension_semantics=("arbitrary", "arbitrary"),
                                             vmem_limit_bytes=VMEM_LIMIT),
        name="attn",
    )(sinks, q, kv, kv, kv_meta, b0, b1)


def _out_ffn_kernel(h_ref, yr_ref, ya_ref, wo_ref, g_ref, wg_ref, wu_ref, wd_ref, gf_ref, o_ref):
    y = jnp.concatenate([yr_ref[...], ya_ref[...]], axis=-1)
    h = h_ref[...] + jnp.dot(y, wo_ref[...], preferred_element_type=F32)
    n = _rmsnorm(h, g_ref[...]).astype(BF16)
    h = h + 0.5 * _swiglu(n, wg_ref, wu_ref, wd_ref)
    o_ref[...] = _rmsnorm(h, gf_ref[...])


def _out_ffn(h, y_rwkv, y_attn, wo, g, wg, wu, wd, gf, tm):
    rows = h.shape[0]
    row = lambda w: pl.BlockSpec((tm, w), lambda i: (i, 0))
    return pl.pallas_call(
        _out_ffn_kernel,
        grid=(rows // tm,),
        in_specs=[row(D_MODEL), row(RWKV_WIDTH), row(ATTN_WIDTH), _const_spec((D_MODEL, D_MODEL)),
                  _const_spec((1, D_MODEL)), _const_spec((D_MODEL, D_FF)), _const_spec((D_MODEL, D_FF)),
                  _const_spec((D_FF, D_MODEL)), _const_spec((1, D_MODEL))],
        out_specs=row(D_MODEL),
        out_shape=jax.ShapeDtypeStruct((rows, D_MODEL), F32),
        compiler_params=pltpu.CompilerParams(dimension_semantics=("arbitrary",),
                                             vmem_limit_bytes=VMEM_LIMIT),
        name="out_ffn",
    )(h, y_rwkv, y_attn, wo, g, wg, wu, wd, gf)


def _head_cols(order):
    return np.concatenate([np.arange(h * HEAD, (h + 1) * HEAD) for h in order])


def _pack_w_in(w_in, b_attn):
    rwkv_cols = RKV_COLS + LORA_COLS
    pad = jnp.zeros((D_MODEL, LORA_PAD - LORA_COLS), w_in.dtype)
    attn_cols = np.concatenate([_head_cols(ATTN_HEAD_ORDER), np.arange(ATTN_WIDTH, ATTN_WIDTH + KV_WIDTH)])
    w = jnp.concatenate([w_in[:, :rwkv_cols], pad, w_in[:, rwkv_cols:][:, attn_cols]], axis=1).astype(BF16)
    b = jnp.concatenate([jnp.zeros((RKV_COLS + LORA_PAD,), F32), b_attn.astype(F32)[attn_cols]])[None]
    return w, b


def _pack_lora(w2, a2, g2):
    w = jnp.zeros((LORA_PAD, 3 * RWKV_WIDTH), F32)
    w = w.at[0:LORA_W, 0:RWKV_WIDTH].set(w2)
    w = w.at[LORA_W:LORA_W + LORA_A, RWKV_WIDTH:2 * RWKV_WIDTH].set(a2)
    w = w.at[LORA_W + LORA_A:LORA_COLS, 2 * RWKV_WIDTH:].set(g2)
    return w


def _block_ones():
    i = jnp.arange(QUAD) // HEAD
    return (i[:, None] == i[None, :]).astype(BF16)


def kernel(x, meta_tokens, ffn1_norm, ffn1_w_gate, ffn1_w_up, ffn1_w_down, mix_norm, w_in, b_attn, rwkv_mu, rwkv_w0, rwkv_w2, rwkv_a0, rwkv_a2, rwkv_g2, rwkv_k_k, rwkv_k_a, rwkv_r_k, rwkv_ln_w, rwkv_ln_b, attn_sinks, w_out, ffn2_norm, ffn2_w_gate, ffn2_w_up, ffn2_w_down, final_norm):
    assert ffn1_norm.shape[0] == 1, "single-layer trunk"
    B, T, D = x.shape
    row = lambda v: v.reshape(1, -1).astype(F32)
    bf = lambda w: w[0].astype(BF16)

    win, b_in = _pack_w_in(w_in[0], b_attn[0])
    mu = rwkv_mu[0].astype(F32)
    mu_lora = jnp.concatenate([mu[RKV_COLS:], jnp.zeros((LORA_PAD - LORA_COLS,), F32)])
    rp = dict(mu_rkv=row(mu[:RKV_COLS]), mu_lora=row(mu_lora),
              wlora=_pack_lora(rwkv_w2[0], rwkv_a2[0], rwkv_g2[0]).astype(BF16), bd=_block_ones(),
              w0=row(rwkv_w0[0]), a0=row(rwkv_a0[0]), k_k=row(rwkv_k_k[0]), k_a=row(rwkv_k_a[0]),
              r_k=row(rwkv_r_k[0]), ln_w=row(rwkv_ln_w[0]), ln_b=row(rwkv_ln_b[0]))
    ffn1 = (row(ffn1_norm[0]), bf(ffn1_w_gate), bf(ffn1_w_up), bf(ffn1_w_down))

    xm = jnp.concatenate([jnp.zeros((CHUNK - N_META, D), F32), meta_tokens.astype(F32)], axis=0)
    _, zrkv_m, zlora_m, _, kv_m = _ffn_in(xm, *ffn1, row(mix_norm[0]), win, b_in, tm=CHUNK)
    _, s_meta = _rwkv(zrkv_m[None], zlora_m[None], jnp.zeros((HEAD, RWKV_WIDTH), F32),
                      jnp.zeros((CHUNK, RKV_COLS), F32), jnp.zeros((CHUNK, LORA_PAD), F32), rp, nc=1)

    xf = x.reshape(B * T, D)
    h1, zrkv, zlora, q, kv = _ffn_in(xf, *ffn1, row(mix_norm[0]), win, b_in, tm=FFN_ROWS)
    y_rwkv, _ = _rwkv(zrkv.reshape(B, T, -1), zlora.reshape(B, T, -1), s_meta[0], zrkv_m, zlora_m, rp,
                      nc=RWKV_CHUNKS_PER_STEP)
    y_attn = _attn(q.reshape(B, T, -1), kv.reshape(B, T, -1), kv_m, attn_sinks[0].astype(F32))
    wo_rows = np.concatenate([np.arange(RWKV_WIDTH), RWKV_WIDTH + _head_cols(ATTN_HEAD_ORDER)])
    out = _out_ffn(h1, y_rwkv.reshape(B * T, -1), y_attn.reshape(B * T, -1), w_out[0][wo_rows].astype(BF16),
                   row(ffn2_norm[0]), bf(ffn2_w_gate), bf(ffn2_w_up), bf(ffn2_w_down),
                   row(final_norm), tm=FFN_ROWS)
    return out.reshape(B, T, D)
```

```python
import functools

import jax
import jax.numpy as jnp
import numpy as np
from jax import lax
from jax.experimental import pallas as pl
from jax.experimental.pallas import tpu as pltpu

F32 = jnp.float32
BF16 = jnp.bfloat16

D_MODEL = 1024
D_FF = 2816
N_META = 16
NORM_EPS = 1e-5
CHUNK = 64

HEAD = 64
RWKV_HEADS = 8
RWKV_WIDTH = 512
LORA_W, LORA_A, LORA_G = 32, 32, 96
LORA_COLS = LORA_W + LORA_A + LORA_G
LORA_PAD = 256
GN_EPS = 64e-5

ATTN_HEADS = 8
ATTN_KV_HEADS = 2
ATTN_GROUP = 4
ATTN_WIDTH = 512
KV_WIDTH = 2 * ATTN_KV_HEADS * HEAD
WIN_CHUNKS = 2
NEG_INF = -1e30
LOG2E = float(np.log2(np.e))

RKV_COLS = 3 * RWKV_WIDTH
Z_COLS = RKV_COLS + LORA_PAD + ATTN_WIDTH + KV_WIDTH

RWKV_CHUNKS_PER_STEP = 8
FFN_ROWS = 512
FF_CHUNK = 256
VMEM_LIMIT = 56 * 1024 * 1024


def _const_spec(shape):
    nd = len(shape)
    return pl.BlockSpec(shape, lambda *_: (0,) * nd, pipeline_mode=pl.Buffered(1))


def _rmsnorm(x, g):
    return x * lax.rsqrt(jnp.mean(x * x, axis=-1, keepdims=True) + NORM_EPS) * g


def _swiglu(n, wg_ref, wu_ref, wd_ref):
    acc = jnp.zeros((n.shape[0], D_MODEL), F32)
    for c in range(D_FF // FF_CHUNK):
        sl = slice(c * FF_CHUNK, (c + 1) * FF_CHUNK)
        g = jnp.dot(n, wg_ref[:, sl], preferred_element_type=F32)
        u = jnp.dot(n, wu_ref[:, sl], preferred_element_type=F32)
        a = (g * jax.nn.sigmoid(g) * u).astype(BF16)
        acc = acc + jnp.dot(a, wd_ref[sl, :], preferred_element_type=F32)
    return acc


def _ffn_in_kernel(x_ref, g1_ref, wg_ref, wu_ref, wd_ref, g2_ref, win_ref, bin_ref,
                   h_ref, zrkv_ref, zlora_ref, q_ref, kv_ref):
    x = x_ref[...]
    n1 = _rmsnorm(x, g1_ref[...]).astype(BF16)
    h = x + 0.5 * _swiglu(n1, wg_ref, wu_ref, wd_ref)
    h_ref[...] = h
    n2 = _rmsnorm(h, g2_ref[...]).astype(BF16)
    z = jnp.dot(n2, win_ref[...], preferred_element_type=F32) + bin_ref[...]
    o = 0
    for ref, w in ((zrkv_ref, RKV_COLS), (zlora_ref, LORA_PAD), (q_ref, ATTN_WIDTH), (kv_ref, KV_WIDTH)):
        ref[...] = z[:, o:o + w].astype(ref.dtype)
        o += w


def _ffn_in(x, g1, wg, wu, wd, g2, win, b_in, tm):
    rows = x.shape[0]
    row = lambda w: pl.BlockSpec((tm, w), lambda i: (i, 0))
    return pl.pallas_call(
        _ffn_in_kernel,
        grid=(rows // tm,),
        in_specs=[row(D_MODEL), _const_spec((1, D_MODEL)), _const_spec((D_MODEL, D_FF)),
                  _const_spec((D_MODEL, D_FF)), _const_spec((D_FF, D_MODEL)), _const_spec((1, D_MODEL)),
                  _const_spec((D_MODEL, Z_COLS)), _const_spec((1, Z_COLS))],
        out_specs=[row(D_MODEL), row(RKV_COLS), row(LORA_PAD), row(ATTN_WIDTH), row(KV_WIDTH)],
        out_shape=[jax.ShapeDtypeStruct((rows, w), t)
                   for w, t in ((D_MODEL, F32), (RKV_COLS, F32), (LORA_PAD, F32), (ATTN_WIDTH, BF16),
                                (KV_WIDTH, BF16))],
        compiler_params=pltpu.CompilerParams(dimension_semantics=("arbitrary",),
                                             vmem_limit_bytes=VMEM_LIMIT),
        name="ffn_in",
    )(x, g1, wg, wu, wd, g2, win, b_in)


QUAD = 4 * HEAD
N_QUADS = RWKV_WIDTH // QUAD


def _blockdiag(x, bd_ref):
    return jnp.concatenate([x * bd_ref[h * HEAD:(h + 1) * HEAD, :] for h in range(4)], axis=0)


def _mm(lhs, rhs, bd_ref, nt=False):
    many = isinstance(lhs, (list, tuple))
    l = jnp.concatenate([x.astype(BF16) for x in lhs], axis=0) if many else lhs.astype(BF16)
    dims = (((1,), (1,)), ((), ())) if nt else (((1,), (0,)), ((), ()))
    out = lax.dot_general(l, _blockdiag(rhs.astype(BF16), bd_ref), dims, preferred_element_type=F32)
    return [out[i * CHUNK:(i + 1) * CHUNK] for i in range(len(lhs))] if many else out


def _mm_tn_diag(lhs_list, rhs_list, head_of_lane):
    lt = jnp.concatenate([l.T for l in lhs_list], axis=1).astype(BF16)
    r = jnp.concatenate(rhs_list, axis=0).astype(BF16)
    full = jnp.dot(lt, r, preferred_element_type=F32)
    out = full[3 * HEAD:4 * HEAD]
    for h in (2, 1, 0):
        out = jnp.where(head_of_lane == h, full[h * HEAD:(h + 1) * HEAD], out)
    return out


def _unit_lower_inverse(a_list, t_idx, j_idx, mm):
    same = lambda n: (t_idx // n) == (j_idx // n)
    eye = (t_idx == j_idx).astype(F32)
    a8 = [jnp.where(same(8), a, 0.0) for a in a_list]
    a2 = [mm(x, x) for x in a8]
    inv = [eye + x for x in a8]
    both = [mm([i, x], x) for i, x in zip(inv, a2)]
    inv = [i + t[0] for i, t in zip(inv, both)]
    inv = [i + mm(i, t[1]) for i, t in zip(inv, both)]
    for n in (16, 32, 64):
        off = same(n) & jnp.logical_not(same(n // 2))
        ci = [mm(jnp.where(off, a, 0.0), i) for a, i in zip(a_list, inv)]
        inv = [i + mm(i, x) for i, x in zip(inv, ci)]
    return inv


def _shift_lerp(z, prev, mu, row0):
    shifted = jnp.where(row0, prev, pltpu.roll(z, 1, axis=0))
    return z + (shifted - z) * mu


def _group_sum(x, bd_ref):
    return jnp.concatenate(
        [jnp.dot(x[:, q * QUAD:(q + 1) * QUAD].astype(BF16), bd_ref[...], preferred_element_type=F32)
         for q in range(N_QUADS)], axis=1)


def _rwkv_kernel(zrkv_ref, zlora_ref, s0_ref, prev_rkv_ref, prev_lora_ref, mu_rkv_ref, mu_lora_ref,
                 wlora_ref, w0_ref, a0_ref, kk_ref, ka_ref, rk_ref, lnw_ref, lnb_ref, bd_ref, tri_ref,
                 y_ref, sfin_ref, s_scr, prkv_scr, plora_scr, *, nc):
    step = pl.program_id(1)

    @pl.when(step == 0)
    def _():
        s_scr[...] = s0_ref[...]
        prkv_scr[...] = prev_rkv_ref[CHUNK - 1:CHUNK, :]
        plora_scr[...] = prev_lora_ref[CHUNK - 1:CHUNK, :]

    T = CHUNK
    R = nc * T
    zrkv_raw = zrkv_ref[0]
    zlora_raw = zlora_ref[0]
    row0 = lax.broadcasted_iota(jnp.int32, (R, 1), 0) == 0
    zrkv = _shift_lerp(zrkv_raw, prkv_scr[...], mu_rkv_ref[...], row0)
    zlora = _shift_lerp(zlora_raw, plora_scr[...], mu_lora_ref[...], row0)
    prkv_scr[...] = zrkv_raw[R - 1:R, :]
    plora_scr[...] = zlora_raw[R - 1:R, :]

    zr = zrkv[:, 0:RWKV_WIDTH]
    zk = zrkv[:, RWKV_WIDTH:2 * RWKV_WIDTH]
    zv = zrkv[:, 2 * RWKV_WIDTH:3 * RWKV_WIDTH]

    lane = lax.broadcasted_iota(jnp.int32, (R, LORA_PAD), 1)
    act = jnp.where(lane < LORA_W, jnp.tanh(zlora),
                    jnp.where(lane < LORA_W + LORA_A, zlora, jax.nn.sigmoid(zlora)))
    lo = jnp.dot(act.astype(BF16), wlora_ref[...], preferred_element_type=F32)
    lw = -jnp.exp(F32(-0.5)) * jax.nn.sigmoid(w0_ref[...] + lo[:, 0:RWKV_WIDTH])
    a = jax.nn.sigmoid(a0_ref[...] + lo[:, RWKV_WIDTH:2 * RWKV_WIDTH])
    g = lo[:, 2 * RWKV_WIDTH:3 * RWKV_WIDTH]

    kkn = zk * kk_ref[...]
    kk = kkn * jnp.minimum(lax.rsqrt(_group_sum(kkn * kkn, bd_ref)), 1e12)
    k = zk * (1.0 + (a - 1.0) * ka_ref[...])
    b = kk * a
    bonus = _group_sum(zr * k * rk_ref[...], bd_ref) * zv

    t_idx = lax.broadcasted_iota(jnp.int32, (T, QUAD), 0)
    l_idx = lax.broadcasted_iota(jnp.int32, (T, QUAD), 1)
    j_idx = l_idx % HEAD
    head_of_lane = l_idx // HEAD
    strict = j_idx < t_idx
    incl = j_idx <= t_idx
    mm = functools.partial(_mm, bd_ref=bd_ref)

    tri = tri_ref[...]
    lw_hi = lw.astype(BF16)
    lw_lo = (lw - lw_hi.astype(F32)).astype(BF16)
    cum = jnp.dot(tri, lw_hi, preferred_element_type=F32) + jnp.dot(tri, lw_lo, preferred_element_type=F32)
    row_of_chunk = lambda r: jnp.concatenate(
        [jnp.broadcast_to(cum[n * T + r:n * T + r + 1], (T, RWKV_WIDTH)) for n in range(nc)], axis=0)
    mid = row_of_chunk(T // 2 - 1)
    end = row_of_chunk(T - 1)
    e = jnp.exp(cum)
    first = lax.broadcasted_iota(jnp.int32, (R, 1), 0) % T == 0
    e_prev = jnp.where(first, 1.0, pltpu.roll(e, 1, axis=0))
    e_mid = jnp.exp(-mid)
    e_neg = jnp.exp(mid - cum)
    e_end = jnp.exp(end - cum)
    at0 = -kk * e_prev
    rt0 = zr * e
    at = at0 * e_mid
    rt = rt0 * e_mid
    bt = b * e_neg
    kt = k * e_neg
    b_end = b * e_end
    k_end = k * e_end

    probs = [(n, q) for n in range(nc) for q in range(N_QUADS)]
    cut = lambda arr, p: arr[p[0] * T:(p[0] + 1) * T, p[1] * QUAD:(p[1] + 1) * QUAD]
    xs = [jnp.concatenate([cut(at, p), cut(rt, p)], axis=0) for p in probs]
    xb = [mm(x, cut(bt, p), nt=True) for x, p in zip(xs, probs)]
    xk = [mm(x, cut(kt, p), nt=True) for x, p in zip(xs, probs)]
    a_ab = [jnp.where(strict, v[:T], 0.0) for v in xb]
    a_rb = [jnp.where(incl, v[T:], 0.0) for v in xb]
    a_ak = [jnp.where(strict, v[:T], 0.0) for v in xk]
    a_rk = [jnp.where(incl, v[T:], 0.0) for v in xk]
    w1yv = [mm([m1, m2], cut(zv, p)) for m1, m2, p in zip(a_ak, a_rk, probs)]
    w1 = [t[0] for t in w1yv]
    yv = [t[1] for t in w1yv]
    inv = _unit_lower_inverse(a_ab, t_idx, j_idx, mm)
    au = [mm(i, cut(at0, p)) for i, p in zip(inv, probs)]
    u0 = [mm(i, w) for i, w in zip(inv, w1)]
    r_eff = [cut(rt0, p) + mm(m, x) for m, x, p in zip(a_rb, au, probs)]
    y0 = [v + mm(m, u) for v, m, u in zip(yv, a_rb, u0)]
    m_all = [_mm_tn_diag([x], [cut(b_end, p)], head_of_lane) for x, p in zip(au, probs)]
    g_all = [_mm_tn_diag([u, cut(zv, p)], [cut(b_end, p), cut(k_end, p)], head_of_lane)
             for u, p in zip(u0, probs)]

    s = [s_scr[:, q * QUAD:(q + 1) * QUAD] for q in range(N_QUADS)]
    y_rows = []
    for n in range(nc):
        ys = []
        for q in range(N_QUADS):
            i = n * N_QUADS + q
            w_tot = e[(n + 1) * T - 1:(n + 1) * T, q * QUAD:(q + 1) * QUAD]
            s_new = s[q] * w_tot + mm(s[q], m_all[i]) + g_all[i]
            ys.append(y0[i] + mm(r_eff[i], s[q], nt=True))
            s[q] = s_new
        y_rows.append(jnp.concatenate(ys, axis=1))
    for q in range(N_QUADS):
        s_scr[:, q * QUAD:(q + 1) * QUAD] = s[q]
    y = jnp.concatenate(y_rows, axis=0)
    yc = y - _group_sum(y, bd_ref) * (1.0 / HEAD)
    var = _group_sum(yc * yc, bd_ref) * (1.0 / HEAD)
    yn = yc * lax.rsqrt(var + GN_EPS) * lnw_ref[...] + lnb_ref[...]
    y_ref[0] = ((yn + bonus) * g).astype(y_ref.dtype)

    @pl.when(step == pl.num_programs(1) - 1)
    def _():
        sfin_ref[0] = s_scr[...]


def _rwkv(zrkv, zlora, s0, prev_rkv, prev_lora, p, nc):
    B, T, _ = zrkv.shape
    rows = nc * CHUNK
    blk = lambda w: pl.BlockSpec((1, rows, w), lambda b, c: (b, c, 0))
    vec = lambda w: _const_spec((1, w))
    state = (HEAD, RWKV_WIDTH)
    r = np.arange(rows)
    tri = jnp.asarray((r[:, None] // CHUNK == r[None, :] // CHUNK) & (r[None, :] <= r[:, None]), BF16)
    return pl.pallas_call(
        functools.partial(_rwkv_kernel, nc=nc),
        grid=(B, T // rows),
        in_specs=[blk(RKV_COLS), blk(LORA_PAD), _const_spec(state), _const_spec((CHUNK, RKV_COLS)),
                  _const_spec((CHUNK, LORA_PAD)),
                  vec(RKV_COLS), vec(LORA_PAD), _const_spec((LORA_PAD, 3 * RWKV_WIDTH))]
                 + [vec(RWKV_WIDTH)] * 7 + [_const_spec((QUAD, QUAD)), _const_spec((rows, rows))],
        out_specs=[blk(RWKV_WIDTH), pl.BlockSpec((1,) + state, lambda b, c: (b, 0, 0))],
        out_shape=[jax.ShapeDtypeStruct((B, T, RWKV_WIDTH), BF16),
                   jax.ShapeDtypeStruct((B,) + state, F32)],
        scratch_shapes=[pltpu.VMEM(state, F32), pltpu.VMEM((1, RKV_COLS), F32),
                        pltpu.VMEM((1, LORA_PAD), F32)],
        compiler_params=pltpu.CompilerParams(dimension_semantics=("arbitrary", "arbitrary"),
                                             vmem_limit_bytes=VMEM_LIMIT),
        name="rwkv",
    )(zrkv, zlora, s0, prev_rkv, prev_lora, p["mu_rkv"], p["mu_lora"], p["wlora"], p["w0"], p["a0"],
      p["k_k"], p["k_a"], p["r_k"], p["ln_w"], p["ln_b"], p["bd"], tri)


N_WIN = (WIN_CHUNKS + 1) * CHUNK
KEY_SLOTS = 256
ATTN_CHUNKS_PER_STEP = 8
PAIR = 2 * HEAD
LOGIT_SCALE = HEAD ** -0.5 * LOG2E
ATTN_HEAD_ORDER = tuple(h for p in range(ATTN_GROUP) for h in (p, ATTN_GROUP + p))


def _attn_bias():
    p = np.arange(ATTN_GROUP)[:, None, None, None]
    i = np.arange(CHUNK)[None, :, None, None]
    j = np.arange(ATTN_KV_HEADS)[None, None, :, None]
    s = np.arange(KEY_SLOTS)[None, None, None, :]
    slope = np.exp2(-8.0 * (ATTN_GROUP * j + p + 1.0) / ATTN_HEADS)
    meta = s >= KEY_SLOTS - N_META
    dist = np.where(meta, N_META + i - (s - (KEY_SLOTS - N_META)), np.abs(WIN_CHUNKS * CHUNK + i - s))
    b0 = np.where((s >= N_WIN) & ~meta, NEG_INF, -slope * dist)
    b1 = np.broadcast_to(np.where(meta, slope * CHUNK, 0.0), b0.shape)
    shape = (ATTN_GROUP * CHUNK, ATTN_KV_HEADS * KEY_SLOTS)
    return (jnp.asarray(LOG2E * b0.reshape(shape), F32), jnp.asarray(LOG2E * b1.reshape(shape), F32))


def _attn_kernel(sink_ref, q_ref, kvp_ref, kvc_ref, kvm_ref, b0_ref, b1_ref, o_ref):
    step = pl.program_id(1)
    nchunks = ATTN_CHUNKS_PER_STEP
    rows = ATTN_GROUP * CHUNK
    kv_rows = jnp.concatenate([kvp_ref[0], kvc_ref[0]], axis=0)
    kvm = kvm_ref[...]
    first_head = lax.broadcasted_iota(jnp.int32, (1, PAIR), 1) < HEAD
    slot = lax.broadcasted_iota(jnp.int32, (1, ATTN_KV_HEADS * KEY_SLOTS), 1) % KEY_SLOTS
    pair_of_row = lax.broadcasted_iota(jnp.int32, (rows, 1), 0) // CHUNK
    sinks = []
    for j in range(ATTN_KV_HEADS):
        col = jnp.zeros((rows, 1), F32)
        for p in range(ATTN_GROUP):
            col = jnp.where(pair_of_row == p, sink_ref[ATTN_GROUP * j + p], col)
        sinks.append(col)

    def per_head(x):
        zero = jnp.zeros_like(x)
        return jnp.concatenate([jnp.where(first_head, x, zero), jnp.where(first_head, zero, x)], axis=0)

    chunk_ids = [step * nchunks + n for n in range(nchunks)]
    keys = [jnp.concatenate([kv_rows[n * CHUNK:n * CHUNK + N_WIN], kvm], axis=0) for n in range(nchunks)]
    qs = [jnp.concatenate([q_ref[0, n * CHUNK:(n + 1) * CHUNK, p * PAIR:(p + 1) * PAIR]
                           for p in range(ATTN_GROUP)], axis=0) for n in range(nchunks)]
    s = [lax.dot_general(q, per_head(kx[:, :PAIR]), (((1,), (1,)), ((), ())), preferred_element_type=F32)
         for q, kx in zip(qs, keys)]
    def add_bias(x, c):
        half = KEY_SLOTS // 2
        parts = []
        for j in range(ATTN_KV_HEADS):
            near = slice(j * KEY_SLOTS, j * KEY_SLOTS + half)
            far = slice(j * KEY_SLOTS + half, (j + 1) * KEY_SLOTS)
            parts.append(x[:, near] + b0_ref[:, near])
            parts.append(x[:, far] + (b0_ref[:, far] - c.astype(F32) * b1_ref[:, far]))
        return jnp.concatenate(parts, axis=1)

    s = [add_bias(x, c) for x, c in zip(s, chunk_ids)]
    s = [jnp.where(slot >= (WIN_CHUNKS - c) * CHUNK, x, NEG_INF) if n < WIN_CHUNKS else x
         for n, (x, c) in enumerate(zip(s, chunk_ids))]
    outs = []
    for n in range(nchunks):
        ps, inv_den = [], []
        for j in range(ATTN_KV_HEADS):
            sj = s[n][:, j * KEY_SLOTS:(j + 1) * KEY_SLOTS]
            m = jnp.maximum(jnp.max(sj, axis=-1, keepdims=True), sinks[j])
            pj = jnp.exp2(sj - m)
            inv_den.append(1.0 / (jnp.sum(pj, axis=-1, keepdims=True) + jnp.exp2(sinks[j] - m)))
            ps.append(pj.astype(BF16))
        o = jnp.dot(jnp.concatenate(ps, axis=1), per_head(keys[n][:, PAIR:]),
                    preferred_element_type=F32)
        o = o * jnp.where(first_head, inv_den[0], inv_den[1])
        outs.append(jnp.concatenate([o[p * CHUNK:(p + 1) * CHUNK] for p in range(ATTN_GROUP)], axis=1))
    o_ref[0] = jnp.concatenate(outs, axis=0).astype(o_ref.dtype)


def _attn(q, kv, kv_meta, sinks):
    B, T, _ = q.shape
    rows = ATTN_CHUNKS_PER_STEP * CHUNK
    back = WIN_CHUNKS * CHUNK
    b0, b1 = _attn_bias()
    return pl.pallas_call(
        _attn_kernel,
        grid=(B, T // rows),
        in_specs=[pl.BlockSpec(memory_space=pltpu.SMEM),
                  pl.BlockSpec((1, rows, ATTN_WIDTH), lambda b, c: (b, c, 0)),
                  pl.BlockSpec((1, back, KV_WIDTH),
                               lambda b, c: (b, jnp.maximum(c * (rows // back) - 1, 0), 0)),
                  pl.BlockSpec((1, rows, KV_WIDTH), lambda b, c: (b, c, 0)),
                  _const_spec((CHUNK, KV_WIDTH)), _const_spec(b0.shape), _const_spec(b1.shape)],
        out_specs=pl.BlockSpec((1, rows, ATTN_WIDTH), lambda b, c: (b, c, 0)),
        out_shape=jax.ShapeDtypeStruct((B, T, ATTN_WIDTH), BF16),
        compiler_params=pltpu.CompilerParams(dimension_semantics=("arbitrary", "arbitrary"),
                                             vmem_limit_bytes=VMEM_LIMIT),
        name="attn",
    )(sinks, q, kv, kv, kv_meta, b0, b1)


def _out_ffn_kernel(h_ref, yr_ref, ya_ref, wo_ref, g_ref, wg_ref, wu_ref, wd_ref, gf_ref, o_ref):
    y = jnp.concatenate([yr_ref[...], ya_ref[...]], axis=-1)
    h = h_ref[...] + jnp.dot(y, wo_ref[...], preferred_element_type=F32)
    n = _rmsnorm(h, g_ref[...]).astype(BF16)
    h = h + 0.5 * _swiglu(n, wg_ref, wu_ref, wd_ref)
    o_ref[...] = _rmsnorm(h, gf_ref[...])


def _out_ffn(h, y_rwkv, y_attn, wo, g, wg, wu, wd, gf, tm):
    rows = h.shape[0]
    row = lambda w: pl.BlockSpec((tm, w), lambda i: (i, 0))
    return pl.pallas_call(
        _out_ffn_kernel,
        grid=(rows // tm,),
        in_specs=[row(D_MODEL), row(RWKV_WIDTH), row(ATTN_WIDTH), _const_spec((D_MODEL, D_MODEL)),
                  _const_spec((1, D_MODEL)), _const_spec((D_MODEL, D_FF)), _const_spec((D_MODEL, D_FF)),
                  _const_spec((D_FF, D_MODEL)), _const_spec((1, D_MODEL))],
        out_specs=row(D_MODEL),
        out_shape=jax.ShapeDtypeStruct((rows, D_MODEL), F32),
        compiler_params=pltpu.CompilerParams(dimension_semantics=("arbitrary",),
                                             vmem_limit_bytes=VMEM_LIMIT),
        name="out_ffn",
    )(h, y_rwkv, y_attn, wo, g, wg, wu, wd, gf)


def _head_cols(order):
    return np.concatenate([np.arange(h * HEAD, (h + 1) * HEAD) for h in order])


def _pack_w_in(w_in, b_attn):
    rwkv_cols = RKV_COLS + LORA_COLS
    pad = jnp.zeros((D_MODEL, LORA_PAD - LORA_COLS), w_in.dtype)
    attn_cols = np.concatenate([_head_cols(ATTN_HEAD_ORDER), np.arange(ATTN_WIDTH, ATTN_WIDTH + KV_WIDTH)])
    scale = np.ones((Z_COLS,), np.float32)
    scale[RKV_COLS + LORA_PAD:RKV_COLS + LORA_PAD + ATTN_WIDTH] = LOGIT_SCALE
    w = jnp.concatenate([w_in[:, :rwkv_cols], pad, w_in[:, rwkv_cols:][:, attn_cols]], axis=1)
    b = jnp.concatenate([jnp.zeros((RKV_COLS + LORA_PAD,), F32), b_attn.astype(F32)[attn_cols]])
    return (w * scale).astype(BF16), (b * scale)[None]


def _pack_lora(w2, a2, g2):
    w = jnp.zeros((LORA_PAD, 3 * RWKV_WIDTH), F32)
    w = w.at[0:LORA_W, 0:RWKV_WIDTH].set(w2)
    w = w.at[LORA_W:LORA_W + LORA_A, RWKV_WIDTH:2 * RWKV_WIDTH].set(a2)
    w = w.at[LORA_W + LORA_A:LORA_COLS, 2 * RWKV_WIDTH:].set(g2)
    return w


def _block_ones():
    i = jnp.arange(QUAD) // HEAD
    return (i[:, None] == i[None, :]).astype(BF16)


def kernel(x, meta_tokens, ffn1_norm, ffn1_w_gate, ffn1_w_up, ffn1_w_down, mix_norm, w_in, b_attn, rwkv_mu, rwkv_w0, rwkv_w2, rwkv_a0, rwkv_a2, rwkv_g2, rwkv_k_k, rwkv_k_a, rwkv_r_k, rwkv_ln_w, rwkv_ln_b, attn_sinks, w_out, ffn2_norm, ffn2_w_gate, ffn2_w_up, ffn2_w_down, final_norm):
    assert ffn1_norm.shape[0] == 1, "single-layer trunk"
    B, T, D = x.shape
    row = lambda v: v.reshape(1, -1).astype(F32)
    bf = lambda w: w[0].astype(BF16)

    win, b_in = _pack_w_in(w_in[0], b_attn[0])
    mu = rwkv_mu[0].astype(F32)
    mu_lora = jnp.concatenate([mu[RKV_COLS:], jnp.zeros((LORA_PAD - LORA_COLS,), F32)])
    rp = dict(mu_rkv=row(mu[:RKV_COLS]), mu_lora=row(mu_lora),
              wlora=_pack_lora(rwkv_w2[0], rwkv_a2[0], rwkv_g2[0]).astype(BF16), bd=_block_ones(),
              w0=row(rwkv_w0[0]), a0=row(rwkv_a0[0]), k_k=row(rwkv_k_k[0]), k_a=row(rwkv_k_a[0]),
              r_k=row(rwkv_r_k[0]), ln_w=row(rwkv_ln_w[0]), ln_b=row(rwkv_ln_b[0]))
    ffn1 = (row(ffn1_norm[0]), bf(ffn1_w_gate), bf(ffn1_w_up), bf(ffn1_w_down))

    xm = jnp.concatenate([jnp.zeros((CHUNK - N_META, D), F32), meta_tokens.astype(F32)], axis=0)
    _, zrkv_m, zlora_m, _, kv_m = _ffn_in(xm, *ffn1, row(mix_norm[0]), win, b_in, tm=CHUNK)
    _, s_meta = _rwkv(zrkv_m[None], zlora_m[None], jnp.zeros((HEAD, RWKV_WIDTH), F32),
                      jnp.zeros((CHUNK, RKV_COLS), F32), jnp.zeros((CHUNK, LORA_PAD), F32), rp, nc=1)

    xf = x.reshape(B * T, D)
    h1, zrkv, zlora, q, kv = _ffn_in(xf, *ffn1, row(mix_norm[0]), win, b_in, tm=FFN_ROWS)
    y_rwkv, _ = _rwkv(zrkv.reshape(B, T, -1), zlora.reshape(B, T, -1), s_meta[0], zrkv_m, zlora_m, rp,
                      nc=RWKV_CHUNKS_PER_STEP)
    y_attn = _attn(q.reshape(B, T, -1), kv.reshape(B, T, -1), kv_m, attn_sinks[0].astype(F32) * LOG2E)
    wo_rows = np.concatenate([np.arange(RWKV_WIDTH), RWKV_WIDTH + _head_cols(ATTN_HEAD_ORDER)])
    out = _out_ffn(h1, y_rwkv.reshape(B * T, -1), y_attn.reshape(B * T, -1), w_out[0][wo_rows].astype(BF16),
                   row(ffn2_norm[0]), bf(ffn2_w_gate), bf(ffn2_w_up), bf(ffn2_w_down),
                   row(final_norm), tm=2 * FFN_ROWS)
    return out.reshape(B, T, D)
```

```python
import functools

import jax
import jax.numpy as jnp
import numpy as np
from jax import lax
from jax.experimental import pallas as pl
from jax.experimental.pallas import tpu as pltpu

F32 = jnp.float32
BF16 = jnp.bfloat16

D_MODEL = 1024
D_FF = 2816
N_META = 16
NORM_EPS = 1e-5
CHUNK = 64

HEAD = 64
RWKV_HEADS = 8
RWKV_WIDTH = 512
LORA_W, LORA_A, LORA_G = 32, 32, 96
LORA_COLS = LORA_W + LORA_A + LORA_G
LORA_PAD = 256
GN_EPS = 64e-5

ATTN_HEADS = 8
ATTN_KV_HEADS = 2
ATTN_GROUP = 4
ATTN_WIDTH = 512
KV_WIDTH = 2 * ATTN_KV_HEADS * HEAD
WIN_CHUNKS = 2
NEG_INF = -1e30
LOG2E = float(np.log2(np.e))

RKV_COLS = 3 * RWKV_WIDTH
Z_COLS = RKV_COLS + LORA_PAD + ATTN_WIDTH + KV_WIDTH

RWKV_CHUNKS_PER_STEP = 8
FFN_ROWS = 512
FF_CHUNK = 256
VMEM_LIMIT = 56 * 1024 * 1024


def _const_spec(shape):
    nd = len(shape)
    return pl.BlockSpec(shape, lambda *_: (0,) * nd, pipeline_mode=pl.Buffered(1))


def _rmsnorm(x, g):
    return x * lax.rsqrt(jnp.mean(x * x, axis=-1, keepdims=True) + NORM_EPS) * g


def _swiglu(n, wg_ref, wu_ref, wd_ref):
    acc = jnp.zeros((n.shape[0], D_MODEL), F32)
    for c in range(D_FF // FF_CHUNK):
        sl = slice(c * FF_CHUNK, (c + 1) * FF_CHUNK)
        g = jnp.dot(n, wg_ref[:, sl], preferred_element_type=F32)
        u = jnp.dot(n, wu_ref[:, sl], preferred_element_type=F32)
        a = (g * jax.nn.sigmoid(g) * u).astype(BF16)
        acc = acc + jnp.dot(a, wd_ref[sl, :], preferred_element_type=F32)
    return acc


def _ffn_in_kernel(x_ref, g1_ref, wg_ref, wu_ref, wd_ref, g2_ref, win_ref, bin_ref,
                   h_ref, zrkv_ref, zlora_ref, q_ref, kv_ref):
    x = x_ref[...]
    n1 = _rmsnorm(x, g1_ref[...]).astype(BF16)
    h = x + 0.5 * _swiglu(n1, wg_ref, wu_ref, wd_ref)
    h_ref[...] = h
    n2 = _rmsnorm(h, g2_ref[...]).astype(BF16)
    z = jnp.dot(n2, win_ref[...], preferred_element_type=F32) + bin_ref[...]
    o = 0
    for ref, w in ((zrkv_ref, RKV_COLS), (zlora_ref, LORA_PAD), (q_ref, ATTN_WIDTH), (kv_ref, KV_WIDTH)):
        cols = z[:, o:o + w]
        ref[...] = (cols * LOGIT_SCALE if ref is q_ref else cols).astype(ref.dtype)
        o += w


def _ffn_in(x, g1, wg, wu, wd, g2, win, b_in, tm):
    rows = x.shape[0]
    row = lambda w: pl.BlockSpec((tm, w), lambda i: (i, 0))
    return pl.pallas_call(
        _ffn_in_kernel,
        grid=(rows // tm,),
        in_specs=[row(D_MODEL), _const_spec((1, D_MODEL)), _const_spec((D_MODEL, D_FF)),
                  _const_spec((D_MODEL, D_FF)), _const_spec((D_FF, D_MODEL)), _const_spec((1, D_MODEL)),
                  _const_spec((D_MODEL, Z_COLS)), _const_spec((1, Z_COLS))],
        out_specs=[row(D_MODEL), row(RKV_COLS), row(LORA_PAD), row(ATTN_WIDTH), row(KV_WIDTH)],
        out_shape=[jax.ShapeDtypeStruct((rows, w), t)
                   for w, t in ((D_MODEL, F32), (RKV_COLS, F32), (LORA_PAD, F32), (ATTN_WIDTH, BF16),
                                (KV_WIDTH, BF16))],
        compiler_params=pltpu.CompilerParams(dimension_semantics=("arbitrary",),
                                             vmem_limit_bytes=VMEM_LIMIT),
        name="ffn_in",
    )(x, g1, wg, wu, wd, g2, win, b_in)


QUAD = 4 * HEAD
N_QUADS = RWKV_WIDTH // QUAD


def _blockdiag(x, bd_ref):
    return jnp.concatenate([x * bd_ref[h * HEAD:(h + 1) * HEAD, :] for h in range(4)], axis=0)


def _mm(lhs, rhs, bd_ref, nt=False):
    many = isinstance(lhs, (list, tuple))
    l = jnp.concatenate([x.astype(BF16) for x in lhs], axis=0) if many else lhs.astype(BF16)
    dims = (((1,), (1,)), ((), ())) if nt else (((1,), (0,)), ((), ()))
    out = lax.dot_general(l, _blockdiag(rhs.astype(BF16), bd_ref), dims, preferred_element_type=F32)
    return [out[i * CHUNK:(i + 1) * CHUNK] for i in range(len(lhs))] if many else out


def _mm_tn_diag(lhs_list, rhs_list, head_of_lane):
    lt = jnp.concatenate([l.T for l in lhs_list], axis=1).astype(BF16)
    r = jnp.concatenate(rhs_list, axis=0).astype(BF16)
    full = jnp.dot(lt, r, preferred_element_type=F32)
    out = full[3 * HEAD:4 * HEAD]
    for h in (2, 1, 0):
        out = jnp.where(head_of_lane == h, full[h * HEAD:(h + 1) * HEAD], out)
    return out


def _unit_lower_inverse(a_list, t_idx, j_idx, mm):
    same = lambda n: (t_idx // n) == (j_idx // n)
    eye = (t_idx == j_idx).astype(F32)
    a8 = [jnp.where(same(8), a, 0.0) for a in a_list]
    a2 = [mm(x, x) for x in a8]
    inv = [eye + x for x in a8]
    both = [mm([i, x], x) for i, x in zip(inv, a2)]
    inv = [i + t[0] for i, t in zip(inv, both)]
    inv = [i + mm(i, t[1]) for i, t in zip(inv, both)]
    for n in (16, 32, 64):
        off = same(n) & jnp.logical_not(same(n // 2))
        ci = [mm(jnp.where(off, a, 0.0), i) for a, i in zip(a_list, inv)]
        inv = [i + mm(i, x) for i, x in zip(inv, ci)]
    return inv


def _shift_lerp(z, prev, mu, row0):
    shifted = jnp.where(row0, prev, pltpu.roll(z, 1, axis=0))
    return z + (shifted - z) * mu


def _group_sum(x, bd_ref):
    return jnp.concatenate(
        [jnp.dot(x[:, q * QUAD:(q + 1) * QUAD].astype(BF16), bd_ref[...], preferred_element_type=F32)
         for q in range(N_QUADS)], axis=1)


def _rwkv_kernel(zrkv_ref, zlora_ref, s0_ref, prev_rkv_ref, prev_lora_ref, mu_rkv_ref, mu_lora_ref,
                 wlora_ref, w0_ref, a0_ref, kk_ref, ka_ref, rk_ref, lnw_ref, lnb_ref, bd_ref, tri_ref,
                 y_ref, sfin_ref, s_scr, prkv_scr, plora_scr, *, nc):
    step = pl.program_id(1)

    @pl.when(step == 0)
    def _():
        s_scr[...] = s0_ref[...]
        prkv_scr[...] = prev_rkv_ref[CHUNK - 1:CHUNK, :]
        plora_scr[...] = prev_lora_ref[CHUNK - 1:CHUNK, :]

    T = CHUNK
    R = nc * T
    zrkv_raw = zrkv_ref[0]
    zlora_raw = zlora_ref[0]
    row0 = lax.broadcasted_iota(jnp.int32, (R, 1), 0) == 0
    zrkv = _shift_lerp(zrkv_raw, prkv_scr[...], mu_rkv_ref[...], row0)
    zlora = _shift_lerp(zlora_raw, plora_scr[...], mu_lora_ref[...], row0)
    prkv_scr[...] = zrkv_raw[R - 1:R, :]
    plora_scr[...] = zlora_raw[R - 1:R, :]

    zr = zrkv[:, 0:RWKV_WIDTH]
    zk = zrkv[:, RWKV_WIDTH:2 * RWKV_WIDTH]
    zv = zrkv[:, 2 * RWKV_WIDTH:3 * RWKV_WIDTH]

    lane = lax.broadcasted_iota(jnp.int32, (R, LORA_PAD), 1)
    act = jnp.where(lane < LORA_W, jnp.tanh(zlora),
                    jnp.where(lane < LORA_W + LORA_A, zlora, jax.nn.sigmoid(zlora)))
    lo = jnp.dot(act.astype(BF16), wlora_ref[...], preferred_element_type=F32)
    lw = -jnp.exp(F32(-0.5)) * jax.nn.sigmoid(w0_ref[...] + lo[:, 0:RWKV_WIDTH])
    a = jax.nn.sigmoid(a0_ref[...] + lo[:, RWKV_WIDTH:2 * RWKV_WIDTH])
    g = lo[:, 2 * RWKV_WIDTH:3 * RWKV_WIDTH]

    kkn = zk * kk_ref[...]
    kk = kkn * jnp.minimum(lax.rsqrt(_group_sum(kkn * kkn, bd_ref)), 1e12)
    k = zk * (1.0 + (a - 1.0) * ka_ref[...])
    b = kk * a
    bonus = _group_sum(zr * k * rk_ref[...], bd_ref) * zv

    t_idx = lax.broadcasted_iota(jnp.int32, (T, QUAD), 0)
    l_idx = lax.broadcasted_iota(jnp.int32, (T, QUAD), 1)
    j_idx = l_idx % HEAD
    head_of_lane = l_idx // HEAD
    strict = j_idx < t_idx
    incl = j_idx <= t_idx
    mm = functools.partial(_mm, bd_ref=bd_ref)

    tri = tri_ref[...]
    lw_hi = lw.astype(BF16)
    lw_lo = (lw - lw_hi.astype(F32)).astype(BF16)
    cum = jnp.dot(tri, lw_hi, preferred_element_type=F32) + jnp.dot(tri, lw_lo, preferred_element_type=F32)
    row_of_chunk = lambda r: jnp.concatenate(
        [jnp.broadcast_to(cum[n * T + r:n * T + r + 1], (T, RWKV_WIDTH)) for n in range(nc)], axis=0)
    mid = row_of_chunk(T // 2 - 1)
    end = row_of_chunk(T - 1)
    e = jnp.exp(cum)
    first = lax.broadcasted_iota(jnp.int32, (R, 1), 0) % T == 0
    e_prev = jnp.where(first, 1.0, pltpu.roll(e, 1, axis=0))
    e_mid = jnp.exp(-mid)
    e_neg = jnp.exp(mid - cum)
    e_end = jnp.exp(end - cum)
    at0 = -kk * e_prev
    rt0 = zr * e
    at = at0 * e_mid
    rt = rt0 * e_mid
    bt = b * e_neg
    kt = k * e_neg
    b_end = b * e_end
    k_end = k * e_end

    probs = [(n, q) for n in range(nc) for q in range(N_QUADS)]
    cut = lambda arr, p: arr[p[0] * T:(p[0] + 1) * T, p[1] * QUAD:(p[1] + 1) * QUAD]
    xs = [jnp.concatenate([cut(at, p), cut(rt, p)], axis=0) for p in probs]
    xb = [mm(x, cut(bt, p), nt=True) for x, p in zip(xs, probs)]
    xk = [mm(x, cut(kt, p), nt=True) for x, p in zip(xs, probs)]
    a_ab = [jnp.where(strict, v[:T], 0.0) for v in xb]
    a_rb = [jnp.where(incl, v[T:], 0.0) for v in xb]
    a_ak = [jnp.where(strict, v[:T], 0.0) for v in xk]
    a_rk = [jnp.where(incl, v[T:], 0.0) for v in xk]
    w1yv = [mm([m1, m2], cut(zv, p)) for m1, m2, p in zip(a_ak, a_rk, probs)]
    w1 = [t[0] for t in w1yv]
    yv = [t[1] for t in w1yv]
    inv = _unit_lower_inverse(a_ab, t_idx, j_idx, mm)
    au = [mm(i, cut(at0, p)) for i, p in zip(inv, probs)]
    u0 = [mm(i, w) for i, w in zip(inv, w1)]
    r_eff = [cut(rt0, p) + mm(m, x) for m, x, p in zip(a_rb, au, probs)]
    y0 = [v + mm(m, u) for v, m, u in zip(yv, a_rb, u0)]
    m_all = [_mm_tn_diag([x], [cut(b_end, p)], head_of_lane) for x, p in zip(au, probs)]
    g_all = [_mm_tn_diag([u, cut(zv, p)], [cut(b_end, p), cut(k_end, p)], head_of_lane)
             for u, p in zip(u0, probs)]

    s = [s_scr[:, q * QUAD:(q + 1) * QUAD] for q in range(N_QUADS)]
    y_rows = []
    for n in range(nc):
        ys = []
        for q in range(N_QUADS):
            i = n * N_QUADS + q
            w_tot = e[(n + 1) * T - 1:(n + 1) * T, q * QUAD:(q + 1) * QUAD]
            s_new = s[q] * w_tot + mm(s[q], m_all[i]) + g_all[i]
            ys.append(y0[i] + mm(r_eff[i], s[q], nt=True))
            s[q] = s_new
        y_rows.append(jnp.concatenate(ys, axis=1))
    for q in range(N_QUADS):
        s_scr[:, q * QUAD:(q + 1) * QUAD] = s[q]
    y = jnp.concatenate(y_rows, axis=0)
    yc = y - _group_sum(y, bd_ref) * (1.0 / HEAD)
    var = _group_sum(yc * yc, bd_ref) * (1.0 / HEAD)
    yn = yc * lax.rsqrt(var + GN_EPS) * lnw_ref[...] + lnb_ref[...]
    y_ref[0] = ((yn + bonus) * g).astype(y_ref.dtype)

    @pl.when(step == pl.num_programs(1) - 1)
    def _():
        sfin_ref[0] = s_scr[...]


def _rwkv(zrkv, zlora, s0, prev_rkv, prev_lora, p, nc):
    B, T, _ = zrkv.shape
    rows = nc * CHUNK
    blk = lambda w: pl.BlockSpec((1, rows, w), lambda b, c: (b, c, 0))
    vec = lambda w: _const_spec((1, w))
    state = (HEAD, RWKV_WIDTH)
    r = np.arange(rows)
    tri = jnp.asarray((r[:, None] // CHUNK == r[None, :] // CHUNK) & (r[None, :] <= r[:, None]), BF16)
    return pl.pallas_call(
        functools.partial(_rwkv_kernel, nc=nc),
        grid=(B, T // rows),
        in_specs=[blk(RKV_COLS), blk(LORA_PAD), _const_spec(state), _const_spec((CHUNK, RKV_COLS)),
                  _const_spec((CHUNK, LORA_PAD)),
                  vec(RKV_COLS), vec(LORA_PAD), _const_spec((LORA_PAD, 3 * RWKV_WIDTH))]
                 + [vec(RWKV_WIDTH)] * 7 + [_const_spec((QUAD, QUAD)), _const_spec((rows, rows))],
        out_specs=[blk(RWKV_WIDTH), pl.BlockSpec((1,) + state, lambda b, c: (b, 0, 0))],
        out_shape=[jax.ShapeDtypeStruct((B, T, RWKV_WIDTH), BF16),
                   jax.ShapeDtypeStruct((B,) + state, F32)],
        scratch_shapes=[pltpu.VMEM(state, F32), pltpu.VMEM((1, RKV_COLS), F32),
                        pltpu.VMEM((1, LORA_PAD), F32)],
        compiler_params=pltpu.CompilerParams(dimension_semantics=("arbitrary", "arbitrary"),
                                             vmem_limit_bytes=VMEM_LIMIT),
        name="rwkv",
    )(zrkv, zlora, s0, prev_rkv, prev_lora, p["mu_rkv"], p["mu_lora"], p["wlora"], p["w0"], p["a0"],
      p["k_k"], p["k_a"], p["r_k"], p["ln_w"], p["ln_b"], p["bd"], tri)


N_WIN = (WIN_CHUNKS + 1) * CHUNK
KEY_SLOTS = 256
ATTN_CHUNKS_PER_STEP = 8
PAIR = 2 * HEAD
LOGIT_SCALE = HEAD ** -0.5 * LOG2E
ATTN_HEAD_ORDER = tuple(h for p in range(ATTN_GROUP) for h in (p, ATTN_GROUP + p))


def _attn_bias():
    p = np.arange(ATTN_GROUP)[:, None, None, None]
    i = np.arange(CHUNK)[None, :, None, None]
    j = np.arange(ATTN_KV_HEADS)[None, None, :, None]
    s = np.arange(KEY_SLOTS)[None, None, None, :]
    slope = np.exp2(-8.0 * (ATTN_GROUP * j + p + 1.0) / ATTN_HEADS)
    meta = s >= KEY_SLOTS - N_META
    dist = np.where(meta, N_META + i - (s - (KEY_SLOTS - N_META)), np.abs(WIN_CHUNKS * CHUNK + i - s))
    b0 = np.where((s >= N_WIN) & ~meta, NEG_INF, -slope * dist)
    b1 = np.broadcast_to(np.where(meta, slope * CHUNK, 0.0), b0.shape)
    shape = (ATTN_GROUP * CHUNK, ATTN_KV_HEADS * KEY_SLOTS)
    return (jnp.asarray(LOG2E * b0.reshape(shape), F32), jnp.asarray(LOG2E * b1.reshape(shape), F32))


def _attn_kernel(sink_ref, q_ref, kvp_ref, kvc_ref, kvm_ref, b0_ref, b1_ref, o_ref):
    step = pl.program_id(1)
    nchunks = ATTN_CHUNKS_PER_STEP
    rows = ATTN_GROUP * CHUNK
    kv_rows = jnp.concatenate([kvp_ref[0], kvc_ref[0]], axis=0)
    kvm = kvm_ref[...]
    first_head = lax.broadcasted_iota(jnp.int32, (1, PAIR), 1) < HEAD
    slot = lax.broadcasted_iota(jnp.int32, (1, ATTN_KV_HEADS * KEY_SLOTS), 1) % KEY_SLOTS
    pair_of_row = lax.broadcasted_iota(jnp.int32, (rows, 1), 0) // CHUNK
    sinks = []
    for j in range(ATTN_KV_HEADS):
        col = jnp.zeros((rows, 1), F32)
        for p in range(ATTN_GROUP):
            col = jnp.where(pair_of_row == p, sink_ref[ATTN_GROUP * j + p], col)
        sinks.append(col)

    def per_head(x):
        zero = jnp.zeros_like(x)
        return jnp.concatenate([jnp.where(first_head, x, zero), jnp.where(first_head, zero, x)], axis=0)

    chunk_ids = [step * nchunks + n for n in range(nchunks)]
    keys = [jnp.concatenate([kv_rows[n * CHUNK:n * CHUNK + N_WIN], kvm], axis=0) for n in range(nchunks)]
    qs = [jnp.concatenate([q_ref[0, n * CHUNK:(n + 1) * CHUNK, p * PAIR:(p + 1) * PAIR]
                           for p in range(ATTN_GROUP)], axis=0) for n in range(nchunks)]
    s = [lax.dot_general(q, per_head(kx[:, :PAIR]), (((1,), (1,)), ((), ())), preferred_element_type=F32)
         for q, kx in zip(qs, keys)]
    def add_bias(x, c):
        half = KEY_SLOTS // 2
        parts = []
        for j in range(ATTN_KV_HEADS):
            near = slice(j * KEY_SLOTS, j * KEY_SLOTS + half)
            far = slice(j * KEY_SLOTS + half, (j + 1) * KEY_SLOTS)
            parts.append(x[:, near] + b0_ref[:, near])
            parts.append(x[:, far] + (b0_ref[:, far] - c.astype(F32) * b1_ref[:, far]))
        return jnp.concatenate(parts, axis=1)

    s = [add_bias(x, c) for x, c in zip(s, chunk_ids)]
    s = [jnp.where(slot >= (WIN_CHUNKS - c) * CHUNK, x, NEG_INF) if n < WIN_CHUNKS else x
         for n, (x, c) in enumerate(zip(s, chunk_ids))]
    outs = []
    for n in range(nchunks):
        ps, inv_den = [], []
        for j in range(ATTN_KV_HEADS):
            sj = s[n][:, j * KEY_SLOTS:(j + 1) * KEY_SLOTS]
            m = jnp.maximum(jnp.max(sj, axis=-1, keepdims=True), sinks[j])
            pj = jnp.exp2(sj - m)
            inv_den.append(1.0 / (jnp.sum(pj, axis=-1, keepdims=True) + jnp.exp2(sinks[j] - m)))
            ps.append(pj.astype(BF16))
        o = jnp.dot(jnp.concatenate(ps, axis=1), per_head(keys[n][:, PAIR:]),
                    preferred_element_type=F32)
        o = o * jnp.where(first_head, inv_den[0], inv_den[1])
        outs.append(jnp.concatenate([o[p * CHUNK:(p + 1) * CHUNK] for p in range(ATTN_GROUP)], axis=1))
    o_ref[0] = jnp.concatenate(outs, axis=0).astype(o_ref.dtype)


def _attn(q, kv, kv_meta, sinks):
    B, T, _ = q.shape
    rows = ATTN_CHUNKS_PER_STEP * CHUNK
    back = WIN_CHUNKS * CHUNK
    b0, b1 = _attn_bias()
    return pl.pallas_call(
        _attn_kernel,
        grid=(B, T // rows),
        in_specs=[pl.BlockSpec(memory_space=pltpu.SMEM),
                  pl.BlockSpec((1, rows, ATTN_WIDTH), lambda b, c: (b, c, 0)),
                  pl.BlockSpec((1, back, KV_WIDTH),
                               lambda b, c: (b, jnp.maximum(c * (rows // back) - 1, 0), 0)),
                  pl.BlockSpec((1, rows, KV_WIDTH), lambda b, c: (b, c, 0)),
                  _const_spec((CHUNK, KV_WIDTH)), _const_spec(b0.shape), _const_spec(b1.shape)],
        out_specs=pl.BlockSpec((1, rows, ATTN_WIDTH), lambda b, c: (b, c, 0)),
        out_shape=jax.ShapeDtypeStruct((B, T, ATTN_WIDTH), BF16),
        compiler_params=pltpu.CompilerParams(dimension_semantics=("arbitrary", "arbitrary"),
                                             vmem_limit_bytes=VMEM_LIMIT),
        name="attn",
    )(sinks, q, kv, kv, kv_meta, b0, b1)


def _out_ffn_kernel(h_ref, yr_ref, ya_ref, wo_ref, g_ref, wg_ref, wu_ref, wd_ref, gf_ref, o_ref):
    y = jnp.concatenate([yr_ref[...], ya_ref[...]], axis=-1)
    h = h_ref[...] + jnp.dot(y, wo_ref[...], preferred_element_type=F32)
    n = _rmsnorm(h, g_ref[...]).astype(BF16)
    h = h + 0.5 * _swiglu(n, wg_ref, wu_ref, wd_ref)
    o_ref[...] = _rmsnorm(h, gf_ref[...])


def _out_ffn(h, y_rwkv, y_attn, wo, g, wg, wu, wd, gf, tm):
    rows = h.shape[0]
    row = lambda w: pl.BlockSpec((tm, w), lambda i: (i, 0))
    return pl.pallas_call(
        _out_ffn_kernel,
        grid=(rows // tm,),
        in_specs=[row(D_MODEL), row(RWKV_WIDTH), row(ATTN_WIDTH), _const_spec((D_MODEL, D_MODEL)),
                  _const_spec((1, D_MODEL)), _const_spec((D_MODEL, D_FF)), _const_spec((D_MODEL, D_FF)),
                  _const_spec((D_FF, D_MODEL)), _const_spec((1, D_MODEL))],
        out_specs=row(D_MODEL),
        out_shape=jax.ShapeDtypeStruct((rows, D_MODEL), F32),
        compiler_params=pltpu.CompilerParams(dimension_semantics=("arbitrary",),
                                             vmem_limit_bytes=VMEM_LIMIT),
        name="out_ffn",
    )(h, y_rwkv, y_attn, wo, g, wg, wu, wd, gf)


def _head_cols(order):
    return np.concatenate([np.arange(h * HEAD, (h + 1) * HEAD) for h in order])


def _pack_w_in(w_in, b_attn):
    rwkv_cols = RKV_COLS + LORA_COLS
    pad = jnp.zeros((D_MODEL, LORA_PAD - LORA_COLS), w_in.dtype)
    attn_cols = np.concatenate([_head_cols(ATTN_HEAD_ORDER), np.arange(ATTN_WIDTH, ATTN_WIDTH + KV_WIDTH)])
    w = jnp.concatenate([w_in[:, :rwkv_cols], pad, w_in[:, rwkv_cols + attn_cols]], axis=1).astype(BF16)
    b = jnp.concatenate([jnp.zeros((RKV_COLS + LORA_PAD,), F32), b_attn.astype(F32)[attn_cols]])[None]
    return w, b


def _pack_lora(w2, a2, g2):
    w = jnp.zeros((LORA_PAD, 3 * RWKV_WIDTH), F32)
    w = w.at[0:LORA_W, 0:RWKV_WIDTH].set(w2)
    w = w.at[LORA_W:LORA_W + LORA_A, RWKV_WIDTH:2 * RWKV_WIDTH].set(a2)
    w = w.at[LORA_W + LORA_A:LORA_COLS, 2 * RWKV_WIDTH:].set(g2)
    return w


def _block_ones():
    i = jnp.arange(QUAD) // HEAD
    return (i[:, None] == i[None, :]).astype(BF16)


def kernel(x, meta_tokens, ffn1_norm, ffn1_w_gate, ffn1_w_up, ffn1_w_down, mix_norm, w_in, b_attn, rwkv_mu, rwkv_w0, rwkv_w2, rwkv_a0, rwkv_a2, rwkv_g2, rwkv_k_k, rwkv_k_a, rwkv_r_k, rwkv_ln_w, rwkv_ln_b, attn_sinks, w_out, ffn2_norm, ffn2_w_gate, ffn2_w_up, ffn2_w_down, final_norm):
    assert ffn1_norm.shape[0] == 1, "single-layer trunk"
    B, T, D = x.shape
    row = lambda v: v.reshape(1, -1).astype(F32)
    bf = lambda w: w[0].astype(BF16)

    win, b_in = _pack_w_in(w_in[0], b_attn[0])
    mu = rwkv_mu[0].astype(F32)
    mu_lora = jnp.concatenate([mu[RKV_COLS:], jnp.zeros((LORA_PAD - LORA_COLS,), F32)])
    rp = dict(mu_rkv=row(mu[:RKV_COLS]), mu_lora=row(mu_lora),
              wlora=_pack_lora(rwkv_w2[0], rwkv_a2[0], rwkv_g2[0]).astype(BF16), bd=_block_ones(),
              w0=row(rwkv_w0[0]), a0=row(rwkv_a0[0]), k_k=row(rwkv_k_k[0]), k_a=row(rwkv_k_a[0]),
              r_k=row(rwkv_r_k[0]), ln_w=row(rwkv_ln_w[0]), ln_b=row(rwkv_ln_b[0]))
    ffn1 = (row(ffn1_norm[0]), bf(ffn1_w_gate), bf(ffn1_w_up), bf(ffn1_w_down))

    xm = jnp.concatenate([jnp.zeros((CHUNK - N_META, D), F32), meta_tokens.astype(F32)], axis=0)
    _, zrkv_m, zlora_m, _, kv_m = _ffn_in(xm, *ffn1, row(mix_norm[0]), win, b_in, tm=CHUNK)
    _, s_meta = _rwkv(zrkv_m[None], zlora_m[None], jnp.zeros((HEAD, RWKV_WIDTH), F32),
                      jnp.zeros((CHUNK, RKV_COLS), F32), jnp.zeros((CHUNK, LORA_PAD), F32), rp, nc=1)

    xf = x.reshape(B * T, D)
    h1, zrkv, zlora, q, kv = _ffn_in(xf, *ffn1, row(mix_norm[0]), win, b_in, tm=FFN_ROWS)
    y_rwkv, _ = _rwkv(zrkv.reshape(B, T, -1), zlora.reshape(B, T, -1), s_meta[0], zrkv_m, zlora_m, rp,
                      nc=RWKV_CHUNKS_PER_STEP)
    y_attn = _attn(q.reshape(B, T, -1), kv.reshape(B, T, -1), kv_m, attn_sinks[0].astype(F32) * LOG2E)
    wo_rows = np.concatenate([np.arange(RWKV_WIDTH), RWKV_WIDTH + _head_cols(ATTN_HEAD_ORDER)])
    out = _out_ffn(h1, y_rwkv.reshape(B * T, -1), y_attn.reshape(B * T, -1), w_out[0][wo_rows].astype(BF16),
                   row(ffn2_norm[0]), bf(ffn2_w_gate), bf(ffn2_w_up), bf(ffn2_w_down),
                   row(final_norm), tm=2 * FFN_ROWS)
    return out.reshape(B, T, D)
```

```python
import functools

import jax
import jax.numpy as jnp
import numpy as np
from jax import lax
from jax.experimental import pallas as pl
from jax.experimental.pallas import tpu as pltpu

F32 = jnp.float32
BF16 = jnp.bfloat16

D_MODEL = 1024
D_FF = 2816
N_META = 16
NORM_EPS = 1e-5
CHUNK = 64

HEAD = 64
RWKV_HEADS = 8
RWKV_WIDTH = 512
LORA_W, LORA_A, LORA_G = 32, 32, 96
LORA_COLS = LORA_W + LORA_A + LORA_G
LORA_PAD = 256
GN_EPS = 64e-5

ATTN_HEADS = 8
ATTN_KV_HEADS = 2
ATTN_GROUP = 4
ATTN_WIDTH = 512
KV_WIDTH = 2 * ATTN_KV_HEADS * HEAD
WIN_CHUNKS = 2
NEG_INF = -1e30
LOG2E = float(np.log2(np.e))

RKV_COLS = 3 * RWKV_WIDTH
Z_COLS = RKV_COLS + LORA_PAD + ATTN_WIDTH + KV_WIDTH

RWKV_CHUNKS_PER_STEP = 8
FFN_ROWS = 512
FF_CHUNK = 256
VMEM_LIMIT = 56 * 1024 * 1024


def _const_spec(shape):
    nd = len(shape)
    return pl.BlockSpec(shape, lambda *_: (0,) * nd, pipeline_mode=pl.Buffered(1))


def _rmsnorm(x, g):
    return x * lax.rsqrt(jnp.mean(x * x, axis=-1, keepdims=True) + NORM_EPS) * g


def _swiglu(n, wg_ref, wu_ref, wd_ref):
    acc = jnp.zeros((n.shape[0], D_MODEL), F32)
    for c in range(D_FF // FF_CHUNK):
        sl = slice(c * FF_CHUNK, (c + 1) * FF_CHUNK)
        g = jnp.dot(n, wg_ref[:, sl], preferred_element_type=F32)
        u = jnp.dot(n, wu_ref[:, sl], preferred_element_type=F32)
        a = (g * jax.nn.sigmoid(g) * u).astype(BF16)
        acc = acc + jnp.dot(a, wd_ref[sl, :], preferred_element_type=F32)
    return acc


def _ffn_in_kernel(x_ref, g1_ref, wg_ref, wu_ref, wd_ref, g2_ref, win_ref, bin_ref,
                   h_ref, zrkv_ref, zlora_ref, q_ref, kv_ref):
    x = x_ref[...]
    n1 = _rmsnorm(x, g1_ref[...]).astype(BF16)
    h = x + 0.5 * _swiglu(n1, wg_ref, wu_ref, wd_ref)
    h_ref[...] = h
    n2 = _rmsnorm(h, g2_ref[...]).astype(BF16)
    z = jnp.dot(n2, win_ref[...], preferred_element_type=F32) + bin_ref[...]
    o = 0
    for ref, w in ((zrkv_ref, RKV_COLS), (zlora_ref, LORA_PAD), (q_ref, ATTN_WIDTH), (kv_ref, KV_WIDTH)):
        cols = z[:, o:o + w]
        ref[...] = (cols * LOGIT_SCALE if ref is q_ref else cols).astype(ref.dtype)
        o += w


def _ffn_in(x, g1, wg, wu, wd, g2, win, b_in, tm):
    rows = x.shape[0]
    row = lambda w: pl.BlockSpec((tm, w), lambda i: (i, 0))
    return pl.pallas_call(
        _ffn_in_kernel,
        grid=(rows // tm,),
        in_specs=[row(D_MODEL), _const_spec((1, D_MODEL)), _const_spec((D_MODEL, D_FF)),
                  _const_spec((D_MODEL, D_FF)), _const_spec((D_FF, D_MODEL)), _const_spec((1, D_MODEL)),
                  _const_spec((D_MODEL, Z_COLS)), _const_spec((1, Z_COLS))],
        out_specs=[row(D_MODEL), row(RKV_COLS), row(LORA_PAD), row(ATTN_WIDTH), row(KV_WIDTH)],
        out_shape=[jax.ShapeDtypeStruct((rows, w), t)
                   for w, t in ((D_MODEL, F32), (RKV_COLS, F32), (LORA_PAD, F32), (ATTN_WIDTH, BF16),
                                (KV_WIDTH, BF16))],
        compiler_params=pltpu.CompilerParams(dimension_semantics=("arbitrary",),
                                             vmem_limit_bytes=VMEM_LIMIT),
        name="ffn_in",
    )(x, g1, wg, wu, wd, g2, win, b_in)


QUAD = 4 * HEAD
N_QUADS = RWKV_WIDTH // QUAD


def _blockdiag(x, bd_ref):
    return jnp.concatenate([x * bd_ref[h * HEAD:(h + 1) * HEAD, :] for h in range(4)], axis=0)


def _mm(lhs, rhs, bd_ref, nt=False):
    many = isinstance(lhs, (list, tuple))
    l = jnp.concatenate([x.astype(BF16) for x in lhs], axis=0) if many else lhs.astype(BF16)
    dims = (((1,), (1,)), ((), ())) if nt else (((1,), (0,)), ((), ()))
    out = lax.dot_general(l, _blockdiag(rhs.astype(BF16), bd_ref), dims, preferred_element_type=F32)
    return [out[i * CHUNK:(i + 1) * CHUNK] for i in range(len(lhs))] if many else out


def _mm_tn_diag(lhs_list, rhs_list, head_of_lane):
    lt = jnp.concatenate([l.T for l in lhs_list], axis=1).astype(BF16)
    r = jnp.concatenate(rhs_list, axis=0).astype(BF16)
    full = jnp.dot(lt, r, preferred_element_type=F32)
    out = full[3 * HEAD:4 * HEAD]
    for h in (2, 1, 0):
        out = jnp.where(head_of_lane == h, full[h * HEAD:(h + 1) * HEAD], out)
    return out


def _unit_lower_inverse(a_list, t_idx, j_idx, mm):
    same = lambda n: (t_idx // n) == (j_idx // n)
    eye = (t_idx == j_idx).astype(F32)
    a8 = [jnp.where(same(8), a, 0.0) for a in a_list]
    a2 = [mm(x, x) for x in a8]
    inv = [eye + x for x in a8]
    both = [mm([i, x], x) for i, x in zip(inv, a2)]
    inv = [i + t[0] for i, t in zip(inv, both)]
    inv = [i + mm(i, t[1]) for i, t in zip(inv, both)]
    for n in (16, 32, 64):
        off = same(n) & jnp.logical_not(same(n // 2))
        ci = [mm(jnp.where(off, a, 0.0), i) for a, i in zip(a_list, inv)]
        inv = [i + mm(i, x) for i, x in zip(inv, ci)]
    return inv


def _shift_lerp(z, prev, mu, row0):
    shifted = jnp.where(row0, prev, pltpu.roll(z, 1, axis=0))
    return z + (shifted - z) * mu


def _group_sum(x, bd_ref):
    return jnp.concatenate(
        [jnp.dot(x[:, q * QUAD:(q + 1) * QUAD].astype(BF16), bd_ref[...], preferred_element_type=F32)
         for q in range(N_QUADS)], axis=1)


def _rwkv_kernel(zrkv_ref, zlora_ref, s0_ref, prev_rkv_ref, prev_lora_ref, mu_rkv_ref, mu_lora_ref,
                 wlora_ref, w0_ref, a0_ref, kk_ref, ka_ref, rk_ref, lnw_ref, lnb_ref, bd_ref, tri_ref,
                 y_ref, sfin_ref, s_scr, prkv_scr, plora_scr, *, nc):
    step = pl.program_id(1)

    @pl.when(step == 0)
    def _():
        s_scr[...] = s0_ref[...]
        prkv_scr[...] = prev_rkv_ref[CHUNK - 1:CHUNK, :]
        plora_scr[...] = prev_lora_ref[CHUNK - 1:CHUNK, :]

    T = CHUNK
    R = nc * T
    zrkv_raw = zrkv_ref[0]
    zlora_raw = zlora_ref[0]
    row0 = lax.broadcasted_iota(jnp.int32, (R, 1), 0) == 0
    zrkv = _shift_lerp(zrkv_raw, prkv_scr[...], mu_rkv_ref[...], row0)
    zlora = _shift_lerp(zlora_raw, plora_scr[...], mu_lora_ref[...], row0)
    prkv_scr[...] = zrkv_raw[R - 1:R, :]
    plora_scr[...] = zlora_raw[R - 1:R, :]

    zr = zrkv[:, 0:RWKV_WIDTH]
    zk = zrkv[:, RWKV_WIDTH:2 * RWKV_WIDTH]
    zv = zrkv[:, 2 * RWKV_WIDTH:3 * RWKV_WIDTH]

    lane = lax.broadcasted_iota(jnp.int32, (R, LORA_PAD), 1)
    act = jnp.where(lane < LORA_W, jnp.tanh(zlora),
                    jnp.where(lane < LORA_W + LORA_A, zlora, jax.nn.sigmoid(zlora)))
    lo = jnp.dot(act.astype(BF16), wlora_ref[...], preferred_element_type=F32)
    lw = -jnp.exp(F32(-0.5)) * jax.nn.sigmoid(w0_ref[...] + lo[:, 0:RWKV_WIDTH])
    a = jax.nn.sigmoid(a0_ref[...] + lo[:, RWKV_WIDTH:2 * RWKV_WIDTH])
    g = lo[:, 2 * RWKV_WIDTH:3 * RWKV_WIDTH]

    kkn = zk * kk_ref[...]
    kk = kkn * jnp.minimum(lax.rsqrt(_group_sum(kkn * kkn, bd_ref)), 1e12)
    k = zk * (1.0 + (a - 1.0) * ka_ref[...])
    b = kk * a
    bonus = _group_sum(zr * k * rk_ref[...], bd_ref) * zv

    t_idx = lax.broadcasted_iota(jnp.int32, (T, QUAD), 0)
    l_idx = lax.broadcasted_iota(jnp.int32, (T, QUAD), 1)
    j_idx = l_idx % HEAD
    head_of_lane = l_idx // HEAD
    strict = j_idx < t_idx
    incl = j_idx <= t_idx
    mm = functools.partial(_mm, bd_ref=bd_ref)

    tri = tri_ref[...]
    lw_hi = lw.astype(BF16)
    lw_lo = (lw - lw_hi.astype(F32)).astype(BF16)
    cum = jnp.dot(tri, lw_hi, preferred_element_type=F32) + jnp.dot(tri, lw_lo, preferred_element_type=F32)
    row_of_chunk = lambda r: jnp.concatenate(
        [jnp.broadcast_to(cum[n * T + r:n * T + r + 1], (T, RWKV_WIDTH)) for n in range(nc)], axis=0)
    mid = row_of_chunk(T // 2 - 1)
    end = row_of_chunk(T - 1)
    e = jnp.exp(cum)
    first = lax.broadcasted_iota(jnp.int32, (R, 1), 0) % T == 0
    e_prev = jnp.where(first, 1.0, pltpu.roll(e, 1, axis=0))
    e_mid = jnp.exp(-mid)
    e_neg = jnp.exp(mid - cum)
    e_end = jnp.exp(end - cum)
    at0 = -kk * e_prev
    rt0 = zr * e
    at = at0 * e_mid
    rt = rt0 * e_mid
    bt = b * e_neg
    kt = k * e_neg
    b_end = b * e_end
    k_end = k * e_end

    probs = [(n, q) for n in range(nc) for q in range(N_QUADS)]
    cut = lambda arr, p: arr[p[0] * T:(p[0] + 1) * T, p[1] * QUAD:(p[1] + 1) * QUAD]
    xs = [jnp.concatenate([cut(at, p), cut(rt, p)], axis=0) for p in probs]
    xb = [mm(x, cut(bt, p), nt=True) for x, p in zip(xs, probs)]
    xk = [mm(x, cut(kt, p), nt=True) for x, p in zip(xs, probs)]
    a_ab = [jnp.where(strict, v[:T], 0.0) for v in xb]
    a_rb = [jnp.where(incl, v[T:], 0.0) for v in xb]
    a_ak = [jnp.where(strict, v[:T], 0.0) for v in xk]
    a_rk = [jnp.where(incl, v[T:], 0.0) for v in xk]
    w1yv = [mm([m1, m2], cut(zv, p)) for m1, m2, p in zip(a_ak, a_rk, probs)]
    w1 = [t[0] for t in w1yv]
    yv = [t[1] for t in w1yv]
    inv = _unit_lower_inverse(a_ab, t_idx, j_idx, mm)
    au = [mm(i, cut(at0, p)) for i, p in zip(inv, probs)]
    u0 = [mm(i, w) for i, w in zip(inv, w1)]
    r_eff = [cut(rt0, p) + mm(m, x) for m, x, p in zip(a_rb, au, probs)]
    y0 = [v + mm(m, u) for v, m, u in zip(yv, a_rb, u0)]
    m_all = [_mm_tn_diag([x], [cut(b_end, p)], head_of_lane) for x, p in zip(au, probs)]
    g_all = [_mm_tn_diag([u, cut(zv, p)], [cut(b_end, p), cut(k_end, p)], head_of_lane)
             for u, p in zip(u0, probs)]

    s = [s_scr[:, q * QUAD:(q + 1) * QUAD] for q in range(N_QUADS)]
    y_rows = []
    for n in range(nc):
        ys = []
        for q in range(N_QUADS):
            i = n * N_QUADS + q
            w_tot = e[(n + 1) * T - 1:(n + 1) * T, q * QUAD:(q + 1) * QUAD]
            s_new = s[q] * w_tot + mm(s[q], m_all[i]) + g_all[i]
            ys.append(y0[i] + mm(r_eff[i], s[q], nt=True))
            s[q] = s_new
        y_rows.append(jnp.concatenate(ys, axis=1))
    for q in range(N_QUADS):
        s_scr[:, q * QUAD:(q + 1) * QUAD] = s[q]
    y = jnp.concatenate(y_rows, axis=0)
    yc = y - _group_sum(y, bd_ref) * (1.0 / HEAD)
    var = _group_sum(yc * yc, bd_ref) * (1.0 / HEAD)
    yn = yc * lax.rsqrt(var + GN_EPS) * lnw_ref[...] + lnb_ref[...]
    y_ref[0] = ((yn + bonus) * g).astype(y_ref.dtype)

    @pl.when(step == pl.num_programs(1) - 1)
    def _():
        sfin_ref[0] = s_scr[...]


def _rwkv(zrkv, zlora, s0, prev_rkv, prev_lora, p, nc):
    B, T, _ = zrkv.shape
    rows = nc * CHUNK
    blk = lambda w: pl.BlockSpec((1, rows, w), lambda b, c: (b, c, 0))
    vec = lambda w: _const_spec((1, w))
    state = (HEAD, RWKV_WIDTH)
    r = np.arange(rows)
    tri = jnp.asarray((r[:, None] // CHUNK == r[None, :] // CHUNK) & (r[None, :] <= r[:, None]), BF16)
    return pl.pallas_call(
        functools.partial(_rwkv_kernel, nc=nc),
        grid=(B, T // rows),
        in_specs=[blk(RKV_COLS), blk(LORA_PAD), _const_spec(state), _const_spec((CHUNK, RKV_COLS)),
                  _const_spec((CHUNK, LORA_PAD)),
                  vec(RKV_COLS), vec(LORA_PAD), _const_spec((LORA_PAD, 3 * RWKV_WIDTH))]
                 + [vec(RWKV_WIDTH)] * 7 + [_const_spec((QUAD, QUAD)), _const_spec((rows, rows))],
        out_specs=[blk(RWKV_WIDTH), pl.BlockSpec((1,) + state, lambda b, c: (b, 0, 0))],
        out_shape=[jax.ShapeDtypeStruct((B, T, RWKV_WIDTH), BF16),
                   jax.ShapeDtypeStruct((B,) + state, F32)],
        scratch_shapes=[pltpu.VMEM(state, F32), pltpu.VMEM((1, RKV_COLS), F32),
                        pltpu.VMEM((1, LORA_PAD), F32)],
        compiler_params=pltpu.CompilerParams(dimension_semantics=("arbitrary", "arbitrary"),
                                             vmem_limit_bytes=VMEM_LIMIT),
        name="rwkv",
    )(zrkv, zlora, s0, prev_rkv, prev_lora, p["mu_rkv"], p["mu_lora"], p["wlora"], p["w0"], p["a0"],
      p["k_k"], p["k_a"], p["r_k"], p["ln_w"], p["ln_b"], p["bd"], tri)


N_WIN = (WIN_CHUNKS + 1) * CHUNK
KEY_SLOTS = 256
ATTN_CHUNKS_PER_STEP = 8
PAIR = 2 * HEAD
LOGIT_SCALE = HEAD ** -0.5 * LOG2E
ATTN_HEAD_ORDER = tuple(h for p in range(ATTN_GROUP) for h in (p, ATTN_GROUP + p))


def _attn_bias():
    p = np.arange(ATTN_GROUP)[:, None, None, None]
    i = np.arange(CHUNK)[None, :, None, None]
    j = np.arange(ATTN_KV_HEADS)[None, None, :, None]
    s = np.arange(KEY_SLOTS)[None, None, None, :]
    slope = np.exp2(-8.0 * (ATTN_GROUP * j + p + 1.0) / ATTN_HEADS)
    meta = s >= KEY_SLOTS - N_META
    dist = np.where(meta, N_META + i - (s - (KEY_SLOTS - N_META)), np.abs(WIN_CHUNKS * CHUNK + i - s))
    b0 = np.where((s >= N_WIN) & ~meta, NEG_INF, -slope * dist)
    b1 = np.broadcast_to(np.where(meta, slope * CHUNK, 0.0), b0.shape)
    shape = (ATTN_GROUP * CHUNK, ATTN_KV_HEADS * KEY_SLOTS)
    return (jnp.asarray(LOG2E * b0.reshape(shape), F32), jnp.asarray(LOG2E * b1.reshape(shape), F32))


def _attn_kernel(sink_ref, q_ref, kvp_ref, kvc_ref, kvm_ref, b0_ref, b1_ref, o_ref):
    step = pl.program_id(1)
    nchunks = ATTN_CHUNKS_PER_STEP
    rows = ATTN_GROUP * CHUNK
    kv_rows = jnp.concatenate([kvp_ref[0], kvc_ref[0]], axis=0)
    kvm = kvm_ref[...]
    first_head = lax.broadcasted_iota(jnp.int32, (1, PAIR), 1) < HEAD
    slot = lax.broadcasted_iota(jnp.int32, (1, ATTN_KV_HEADS * KEY_SLOTS), 1) % KEY_SLOTS
    pair_of_row = lax.broadcasted_iota(jnp.int32, (rows, 1), 0) // CHUNK
    sinks = []
    for j in range(ATTN_KV_HEADS):
        col = jnp.zeros((rows, 1), F32)
        for p in range(ATTN_GROUP):
            col = jnp.where(pair_of_row == p, sink_ref[ATTN_GROUP * j + p], col)
        sinks.append(col)

    def per_head(x):
        zero = jnp.zeros_like(x)
        return jnp.concatenate([jnp.where(first_head, x, zero), jnp.where(first_head, zero, x)], axis=0)

    chunk_ids = [step * nchunks + n for n in range(nchunks)]
    keys = [jnp.concatenate([kv_rows[n * CHUNK:n * CHUNK + N_WIN], kvm], axis=0) for n in range(nchunks)]
    qs = [jnp.concatenate([q_ref[0, n * CHUNK:(n + 1) * CHUNK, p * PAIR:(p + 1) * PAIR]
                           for p in range(ATTN_GROUP)], axis=0) for n in range(nchunks)]
    s = [lax.dot_general(q, per_head(kx[:, :PAIR]), (((1,), (1,)), ((), ())), preferred_element_type=F32)
         for q, kx in zip(qs, keys)]
    def add_bias(x, c):
        half = KEY_SLOTS // 2
        parts = []
        for j in range(ATTN_KV_HEADS):
            near = slice(j * KEY_SLOTS, j * KEY_SLOTS + half)
            far = slice(j * KEY_SLOTS + half, (j + 1) * KEY_SLOTS)
            parts.append(x[:, near] + b0_ref[:, near])
            parts.append(x[:, far] + (b0_ref[:, far] - c.astype(F32) * b1_ref[:, far]))
        return jnp.concatenate(parts, axis=1)

    s = [add_bias(x, c) for x, c in zip(s, chunk_ids)]
    s = [jnp.where(slot >= (WIN_CHUNKS - c) * CHUNK, x, NEG_INF) if n < WIN_CHUNKS else x
         for n, (x, c) in enumerate(zip(s, chunk_ids))]
    outs = []
    for n in range(nchunks):
        ps, inv_den = [], []
        for j in range(ATTN_KV_HEADS):
            sj = s[n][:, j * KEY_SLOTS:(j + 1) * KEY_SLOTS]
            m = jnp.maximum(jnp.max(sj, axis=-1, keepdims=True), sinks[j])
            pj = jnp.exp2(sj - m)
            inv_den.append(1.0 / (jnp.sum(pj, axis=-1, keepdims=True) + jnp.exp2(sinks[j] - m)))
            ps.append(pj.astype(BF16))
        o = jnp.dot(jnp.concatenate(ps, axis=1), per_head(keys[n][:, PAIR:]),
                    preferred_element_type=F32)
        o = o * jnp.where(first_head, inv_den[0], inv_den[1])
        outs.append(jnp.concatenate([o[p * CHUNK:(p + 1) * CHUNK] for p in range(ATTN_GROUP)], axis=1))
    o_ref[0] = jnp.concatenate(outs, axis=0).astype(o_ref.dtype)


def _attn(q, kv, kv_meta, sinks):
    B, T, _ = q.shape
    rows = ATTN_CHUNKS_PER_STEP * CHUNK
    back = WIN_CHUNKS * CHUNK
    b0, b1 = _attn_bias()
    return pl.pallas_call(
        _attn_kernel,
        grid=(B, T // rows),
        in_specs=[pl.BlockSpec(memory_space=pltpu.SMEM),
                  pl.BlockSpec((1, rows, ATTN_WIDTH), lambda b, c: (b, c, 0)),
                  pl.BlockSpec((1, back, KV_WIDTH),
                               lambda b, c: (b, jnp.maximum(c * (rows // back) - 1, 0), 0)),
                  pl.BlockSpec((1, rows, KV_WIDTH), lambda b, c: (b, c, 0)),
                  _const_spec((CHUNK, KV_WIDTH)), _const_spec(b0.shape), _const_spec(b1.shape)],
        out_specs=pl.BlockSpec((1, rows, ATTN_WIDTH), lambda b, c: (b, c, 0)),
        out_shape=jax.ShapeDtypeStruct((B, T, ATTN_WIDTH), BF16),
        compiler_params=pltpu.CompilerParams(dimension_semantics=("arbitrary", "arbitrary"),
                                             vmem_limit_bytes=VMEM_LIMIT),
        name="attn",
    )(sinks, q, kv, kv, kv_meta, b0, b1)


def _out_ffn_kernel(h_ref, yr_ref, ya_ref, wo_ref, g_ref, wg_ref, wu_ref, wd_ref, gf_ref, o_ref):
    y = jnp.concatenate([yr_ref[...], ya_ref[...]], axis=-1)
    h = h_ref[...] + jnp.dot(y, wo_ref[...], preferred_element_type=F32)
    n = _rmsnorm(h, g_ref[...]).astype(BF16)
    h = h + 0.5 * _swiglu(n, wg_ref, wu_ref, wd_ref)
    o_ref[...] = _rmsnorm(h, gf_ref[...])


def _out_ffn(h, y_rwkv, y_attn, wo, g, wg, wu, wd, gf, tm):
    rows = h.shape[0]
    row = lambda w: pl.BlockSpec((tm, w), lambda i: (i, 0))
    return pl.pallas_call(
        _out_ffn_kernel,
        grid=(rows // tm,),
        in_specs=[row(D_MODEL), row(RWKV_WIDTH), row(ATTN_WIDTH), _const_spec((D_MODEL, D_MODEL)),
                  _const_spec((1, D_MODEL)), _const_spec((D_MODEL, D_FF)), _const_spec((D_MODEL, D_FF)),
                  _const_spec((D_FF, D_MODEL)), _const_spec((1, D_MODEL))],
        out_specs=row(D_MODEL),
        out_shape=jax.ShapeDtypeStruct((rows, D_MODEL), F32),
        compiler_params=pltpu.CompilerParams(dimension_semantics=("arbitrary",),
                                             vmem_limit_bytes=VMEM_LIMIT),
        name="out_ffn",
    )(h, y_rwkv, y_attn, wo, g, wg, wu, wd, gf)


def _head_cols(order):
    return np.concatenate([np.arange(h * HEAD, (h + 1) * HEAD) for h in order])


def _pack_w_in(w_in, b_attn):
    rwkv_cols = RKV_COLS + LORA_COLS
    pad = jnp.zeros((D_MODEL, LORA_PAD - LORA_COLS), BF16)
    attn_cols = np.concatenate([_head_cols(ATTN_HEAD_ORDER), np.arange(ATTN_WIDTH, ATTN_WIDTH + KV_WIDTH)])
    w_in = w_in.astype(BF16)
    w = jnp.concatenate([w_in[:, :rwkv_cols], pad, w_in[:, rwkv_cols + attn_cols]], axis=1)
    b = jnp.concatenate([jnp.zeros((RKV_COLS + LORA_PAD,), F32), b_attn.astype(F32)[attn_cols]])[None]
    return w, b


def _pack_lora(w2, a2, g2):
    w = jnp.zeros((LORA_PAD, 3 * RWKV_WIDTH), F32)
    w = w.at[0:LORA_W, 0:RWKV_WIDTH].set(w2)
    w = w.at[LORA_W:LORA_W + LORA_A, RWKV_WIDTH:2 * RWKV_WIDTH].set(a2)
    w = w.at[LORA_W + LORA_A:LORA_COLS, 2 * RWKV_WIDTH:].set(g2)
    return w


def _block_ones():
    i = jnp.arange(QUAD) // HEAD
    return (i[:, None] == i[None, :]).astype(BF16)


def kernel(x, meta_tokens, ffn1_norm, ffn1_w_gate, ffn1_w_up, ffn1_w_down, mix_norm, w_in, b_attn, rwkv_mu, rwkv_w0, rwkv_w2, rwkv_a0, rwkv_a2, rwkv_g2, rwkv_k_k, rwkv_k_a, rwkv_r_k, rwkv_ln_w, rwkv_ln_b, attn_sinks, w_out, ffn2_norm, ffn2_w_gate, ffn2_w_up, ffn2_w_down, final_norm):
    assert ffn1_norm.shape[0] == 1, "single-layer trunk"
    B, T, D = x.shape
    row = lambda v: v.reshape(1, -1).astype(F32)
    bf = lambda w: w[0].astype(BF16)

    win, b_in = _pack_w_in(w_in[0], b_attn[0])
    mu = rwkv_mu[0].astype(F32)
    mu_lora = jnp.concatenate([mu[RKV_COLS:], jnp.zeros((LORA_PAD - LORA_COLS,), F32)])
    rp = dict(mu_rkv=row(mu[:RKV_COLS]), mu_lora=row(mu_lora),
              wlora=_pack_lora(rwkv_w2[0], rwkv_a2[0], rwkv_g2[0]).astype(BF16), bd=_block_ones(),
              w0=row(rwkv_w0[0]), a0=row(rwkv_a0[0]), k_k=row(rwkv_k_k[0]), k_a=row(rwkv_k_a[0]),
              r_k=row(rwkv_r_k[0]), ln_w=row(rwkv_ln_w[0]), ln_b=row(rwkv_ln_b[0]))
    ffn1 = (row(ffn1_norm[0]), bf(ffn1_w_gate), bf(ffn1_w_up), bf(ffn1_w_down))

    xm = jnp.concatenate([jnp.zeros((CHUNK - N_META, D), F32), meta_tokens.astype(F32)], axis=0)
    _, zrkv_m, zlora_m, _, kv_m = _ffn_in(xm, *ffn1, row(mix_norm[0]), win, b_in, tm=CHUNK)
    _, s_meta = _rwkv(zrkv_m[None], zlora_m[None], jnp.zeros((HEAD, RWKV_WIDTH), F32),
                      jnp.zeros((CHUNK, RKV_COLS), F32), jnp.zeros((CHUNK, LORA_PAD), F32), rp, nc=1)

    xf = x.reshape(B * T, D)
    h1, zrkv, zlora, q, kv = _ffn_in(xf, *ffn1, row(mix_norm[0]), win, b_in, tm=FFN_ROWS)
    y_rwkv, _ = _rwkv(zrkv.reshape(B, T, -1), zlora.reshape(B, T, -1), s_meta[0], zrkv_m, zlora_m, rp,
                      nc=RWKV_CHUNKS_PER_STEP)
    y_attn = _attn(q.reshape(B, T, -1), kv.reshape(B, T, -1), kv_m, attn_sinks[0].astype(F32) * LOG2E)
    wo_rows = np.concatenate([np.arange(RWKV_WIDTH), RWKV_WIDTH + _head_cols(ATTN_HEAD_ORDER)])
    out = _out_ffn(h1, y_rwkv.reshape(B * T, -1), y_attn.reshape(B * T, -1), w_out[0].astype(BF16)[wo_rows],
                   row(ffn2_norm[0]), bf(ffn2_w_gate), bf(ffn2_w_up), bf(ffn2_w_down),
                   row(final_norm), tm=2 * FFN_ROWS)
    return out.reshape(B, T, D)
```

```python
import functools

import jax
import jax.numpy as jnp
import numpy as np
from jax import lax
from jax.experimental import pallas as pl
from jax.experimental.pallas import tpu as pltpu

F32 = jnp.float32
BF16 = jnp.bfloat16

D_MODEL = 1024
D_FF = 2816
N_META = 16
NORM_EPS = 1e-5
CHUNK = 64

HEAD = 64
RWKV_HEADS = 8
RWKV_WIDTH = 512
LORA_W, LORA_A, LORA_G = 32, 32, 96
LORA_COLS = LORA_W + LORA_A + LORA_G
LORA_PAD = 256
GN_EPS = 64e-5

ATTN_HEADS = 8
ATTN_KV_HEADS = 2
ATTN_GROUP = 4
ATTN_WIDTH = 512
KV_WIDTH = 2 * ATTN_KV_HEADS * HEAD
WIN_CHUNKS = 2
NEG_INF = -1e30
LOG2E = float(np.log2(np.e))

RKV_COLS = 3 * RWKV_WIDTH
Z_COLS = RKV_COLS + LORA_PAD + ATTN_WIDTH + KV_WIDTH

RWKV_CHUNKS_PER_STEP = 8
FFN_ROWS = 512
OUT_FFN_ROWS = 2 * FFN_ROWS
FF_CHUNK = 256
VMEM_LIMIT = 56 * 1024 * 1024


def _const_spec(shape):
    nd = len(shape)
    return pl.BlockSpec(shape, lambda *_: (0,) * nd, pipeline_mode=pl.Buffered(1))


def _rmsnorm(x, g):
    return x * lax.rsqrt(jnp.mean(x * x, axis=-1, keepdims=True) + NORM_EPS) * g


def _swiglu(n, wg_ref, wu_ref, wd_ref):
    acc = jnp.zeros((n.shape[0], D_MODEL), F32)
    for c in range(D_FF // FF_CHUNK):
        sl = slice(c * FF_CHUNK, (c + 1) * FF_CHUNK)
        g = jnp.dot(n, wg_ref[:, sl], preferred_element_type=F32)
        u = jnp.dot(n, wu_ref[:, sl], preferred_element_type=F32)
        a = (g * jax.nn.sigmoid(g) * u).astype(BF16)
        acc = acc + jnp.dot(a, wd_ref[sl, :], preferred_element_type=F32)
    return acc


def _ffn_in_kernel(x_ref, g1_ref, wg_ref, wu_ref, wd_ref, g2_ref, win_ref, bin_ref,
                   h_ref, zrkv_ref, zlora_ref, q_ref, kv_ref):
    x = x_ref[...]
    n1 = _rmsnorm(x, g1_ref[...]).astype(BF16)
    h = x + 0.5 * _swiglu(n1, wg_ref, wu_ref, wd_ref)
    h_ref[...] = h
    n2 = _rmsnorm(h, g2_ref[...]).astype(BF16)
    z = jnp.dot(n2, win_ref[...], preferred_element_type=F32) + bin_ref[...]
    o = 0
    for ref, w in ((zrkv_ref, RKV_COLS), (zlora_ref, LORA_PAD), (q_ref, ATTN_WIDTH), (kv_ref, KV_WIDTH)):
        cols = z[:, o:o + w]
        ref[...] = (cols * LOGIT_SCALE if ref is q_ref else cols).astype(ref.dtype)
        o += w


def _ffn_in(x, g1, wg, wu, wd, g2, win, b_in, tm):
    rows = x.shape[0]
    row = lambda w: pl.BlockSpec((tm, w), lambda i: (i, 0))
    return pl.pallas_call(
        _ffn_in_kernel,
        grid=(rows // tm,),
        in_specs=[row(D_MODEL), _const_spec((1, D_MODEL)), _const_spec((D_MODEL, D_FF)),
                  _const_spec((D_MODEL, D_FF)), _const_spec((D_FF, D_MODEL)), _const_spec((1, D_MODEL)),
                  _const_spec((D_MODEL, Z_COLS)), _const_spec((1, Z_COLS))],
        out_specs=[row(D_MODEL), row(RKV_COLS), row(LORA_PAD), row(ATTN_WIDTH), row(KV_WIDTH)],
        out_shape=[jax.ShapeDtypeStruct((rows, w), t)
                   for w, t in ((D_MODEL, F32), (RKV_COLS, F32), (LORA_PAD, F32), (ATTN_WIDTH, BF16),
                                (KV_WIDTH, BF16))],
        compiler_params=pltpu.CompilerParams(dimension_semantics=("arbitrary",),
                                             vmem_limit_bytes=VMEM_LIMIT),
        name="ffn_in",
    )(x, g1, wg, wu, wd, g2, win, b_in)


QUAD = 4 * HEAD
N_QUADS = RWKV_WIDTH // QUAD


def _blockdiag(x, bd_ref):
    return jnp.concatenate([x * bd_ref[h * HEAD:(h + 1) * HEAD, :] for h in range(4)], axis=0)


def _mm(lhs, rhs, bd_ref, nt=False):
    many = isinstance(lhs, (list, tuple))
    l = jnp.concatenate([x.astype(BF16) for x in lhs], axis=0) if many else lhs.astype(BF16)
    dims = (((1,), (1,)), ((), ())) if nt else (((1,), (0,)), ((), ()))
    out = lax.dot_general(l, _blockdiag(rhs.astype(BF16), bd_ref), dims, preferred_element_type=F32)
    return [out[i * CHUNK:(i + 1) * CHUNK] for i in range(len(lhs))] if many else out


def _mm_tn_diag(lhs_list, rhs_list, head_of_lane):
    lt = jnp.concatenate([l.T for l in lhs_list], axis=1).astype(BF16)
    r = jnp.concatenate(rhs_list, axis=0).astype(BF16)
    full = jnp.dot(lt, r, preferred_element_type=F32)
    out = full[3 * HEAD:4 * HEAD]
    for h in (2, 1, 0):
        out = jnp.where(head_of_lane == h, full[h * HEAD:(h + 1) * HEAD], out)
    return out


def _unit_lower_inverse(a_list, t_idx, j_idx, mm):
    same = lambda n: (t_idx // n) == (j_idx // n)
    eye = (t_idx == j_idx).astype(F32)
    a8 = [jnp.where(same(8), a, 0.0) for a in a_list]
    a2 = [mm(x, x) for x in a8]
    inv = [eye + x for x in a8]
    both = [mm([i, x], x) for i, x in zip(inv, a2)]
    inv = [i + t[0] for i, t in zip(inv, both)]
    inv = [i + mm(i, t[1]) for i, t in zip(inv, both)]
    for n in (16, 32, 64):
        off = same(n) & jnp.logical_not(same(n // 2))
        ci = [mm(jnp.where(off, a, 0.0), i) for a, i in zip(a_list, inv)]
        inv = [i + mm(i, x) for i, x in zip(inv, ci)]
    return inv


def _shift_lerp(z, prev, mu, row0):
    shifted = jnp.where(row0, prev, pltpu.roll(z, 1, axis=0))
    return z + (shifted - z) * mu


def _group_sum(x, bd_ref):
    return jnp.concatenate(
        [jnp.dot(x[:, q * QUAD:(q + 1) * QUAD].astype(BF16), bd_ref[...], preferred_element_type=F32)
         for q in range(N_QUADS)], axis=1)


def _rwkv_kernel(zrkv_ref, zlora_ref, s0_ref, prev_rkv_ref, prev_lora_ref, mu_rkv_ref, mu_lora_ref,
                 wlora_ref, w0_ref, a0_ref, kk_ref, ka_ref, rk_ref, lnw_ref, lnb_ref, bd_ref, tri_ref,
                 y_ref, sfin_ref, s_scr, prkv_scr, plora_scr, *, nc):
    step = pl.program_id(1)

    @pl.when(step == 0)
    def _():
        s_scr[...] = s0_ref[...]
        prkv_scr[...] = prev_rkv_ref[CHUNK - 1:CHUNK, :]
        plora_scr[...] = prev_lora_ref[CHUNK - 1:CHUNK, :]

    T = CHUNK
    R = nc * T
    zrkv_raw = zrkv_ref[0]
    zlora_raw = zlora_ref[0]
    row0 = lax.broadcasted_iota(jnp.int32, (R, 1), 0) == 0
    zrkv = _shift_lerp(zrkv_raw, prkv_scr[...], mu_rkv_ref[...], row0)
    zlora = _shift_lerp(zlora_raw, plora_scr[...], mu_lora_ref[...], row0)
    prkv_scr[...] = zrkv_raw[R - 1:R, :]
    plora_scr[...] = zlora_raw[R - 1:R, :]

    zr = zrkv[:, 0:RWKV_WIDTH]
    zk = zrkv[:, RWKV_WIDTH:2 * RWKV_WIDTH]
    zv = zrkv[:, 2 * RWKV_WIDTH:3 * RWKV_WIDTH]

    lane = lax.broadcasted_iota(jnp.int32, (R, LORA_PAD), 1)
    act = jnp.where(lane < LORA_W, jnp.tanh(zlora),
                    jnp.where(lane < LORA_W + LORA_A, zlora, jax.nn.sigmoid(zlora)))
    lo = jnp.dot(act.astype(BF16), wlora_ref[...], preferred_element_type=F32)
    lw = -jnp.exp(F32(-0.5)) * jax.nn.sigmoid(w0_ref[...] + lo[:, 0:RWKV_WIDTH])
    a = jax.nn.sigmoid(a0_ref[...] + lo[:, RWKV_WIDTH:2 * RWKV_WIDTH])
    g = lo[:, 2 * RWKV_WIDTH:3 * RWKV_WIDTH]

    kkn = zk * kk_ref[...]
    kk = kkn * jnp.minimum(lax.rsqrt(_group_sum(kkn * kkn, bd_ref)), 1e12)
    k = zk * (1.0 + (a - 1.0) * ka_ref[...])
    b = kk * a
    bonus = _group_sum(zr * k * rk_ref[...], bd_ref) * zv

    t_idx = lax.broadcasted_iota(jnp.int32, (T, QUAD), 0)
    l_idx = lax.broadcasted_iota(jnp.int32, (T, QUAD), 1)
    j_idx = l_idx % HEAD
    head_of_lane = l_idx // HEAD
    strict = j_idx < t_idx
    incl = j_idx <= t_idx
    mm = functools.partial(_mm, bd_ref=bd_ref)

    tri = tri_ref[...]
    lw_hi = lw.astype(BF16)
    lw_lo = (lw - lw_hi.astype(F32)).astype(BF16)
    cum = jnp.dot(tri, lw_hi, preferred_element_type=F32) + jnp.dot(tri, lw_lo, preferred_element_type=F32)
    row_of_chunk = lambda r: jnp.concatenate(
        [jnp.broadcast_to(cum[n * T + r:n * T + r + 1], (T, RWKV_WIDTH)) for n in range(nc)], axis=0)
    mid = row_of_chunk(T // 2 - 1)
    end = row_of_chunk(T - 1)
    e = jnp.exp(cum)
    first = lax.broadcasted_iota(jnp.int32, (R, 1), 0) % T == 0
    e_prev = jnp.where(first, 1.0, pltpu.roll(e, 1, axis=0))
    e_mid = jnp.exp(-mid)
    e_neg = jnp.exp(mid - cum)
    e_end = jnp.exp(end - cum)
    at0 = -kk * e_prev
    rt0 = zr * e
    at = at0 * e_mid
    rt = rt0 * e_mid
    bt = b * e_neg
    kt = k * e_neg
    b_end = b * e_end
    k_end = k * e_end

    probs = [(n, q) for n in range(nc) for q in range(N_QUADS)]
    cut = lambda arr, p: arr[p[0] * T:(p[0] + 1) * T, p[1] * QUAD:(p[1] + 1) * QUAD]
    xs = [jnp.concatenate([cut(at, p), cut(rt, p)], axis=0) for p in probs]
    xb = [mm(x, cut(bt, p), nt=True) for x, p in zip(xs, probs)]
    xk = [mm(x, cut(kt, p), nt=True) for x, p in zip(xs, probs)]
    a_ab = [jnp.where(strict, v[:T], 0.0) for v in xb]
    a_rb = [jnp.where(incl, v[T:], 0.0) for v in xb]
    a_ak = [jnp.where(strict, v[:T], 0.0) for v in xk]
    a_rk = [jnp.where(incl, v[T:], 0.0) for v in xk]
    w1yv = [mm([m1, m2], cut(zv, p)) for m1, m2, p in zip(a_ak, a_rk, probs)]
    w1 = [t[0] for t in w1yv]
    yv = [t[1] for t in w1yv]
    inv = _unit_lower_inverse(a_ab, t_idx, j_idx, mm)
    au = [mm(i, cut(at0, p)) for i, p in zip(inv, probs)]
    u0 = [mm(i, w) for i, w in zip(inv, w1)]
    r_eff = [cut(rt0, p) + mm(m, x) for m, x, p in zip(a_rb, au, probs)]
    y0 = [v + mm(m, u) for v, m, u in zip(yv, a_rb, u0)]
    m_all = [_mm_tn_diag([x], [cut(b_end, p)], head_of_lane) for x, p in zip(au, probs)]
    g_all = [_mm_tn_diag([u, cut(zv, p)], [cut(b_end, p), cut(k_end, p)], head_of_lane)
             for u, p in zip(u0, probs)]

    s = [s_scr[:, q * QUAD:(q + 1) * QUAD] for q in range(N_QUADS)]
    y_rows = []
    for n in range(nc):
        ys = []
        for q in range(N_QUADS):
            i = n * N_QUADS + q
            w_tot = e[(n + 1) * T - 1:(n + 1) * T, q * QUAD:(q + 1) * QUAD]
            s_new = s[q] * w_tot + mm(s[q], m_all[i]) + g_all[i]
            ys.append(y0[i] + mm(r_eff[i], s[q], nt=True))
            s[q] = s_new
        y_rows.append(jnp.concatenate(ys, axis=1))
    for q in range(N_QUADS):
        s_scr[:, q * QUAD:(q + 1) * QUAD] = s[q]
    y = jnp.concatenate(y_rows, axis=0)
    yc = y - _group_sum(y, bd_ref) * (1.0 / HEAD)
    var = _group_sum(yc * yc, bd_ref) * (1.0 / HEAD)
    yn = yc * lax.rsqrt(var + GN_EPS) * lnw_ref[...] + lnb_ref[...]
    y_ref[0] = ((yn + bonus) * g).astype(y_ref.dtype)

    @pl.when(step == pl.num_programs(1) - 1)
    def _():
        sfin_ref[0] = s_scr[...]


def _rwkv(zrkv, zlora, s0, prev_rkv, prev_lora, p, nc):
    B, T, _ = zrkv.shape
    rows = nc * CHUNK
    blk = lambda w: pl.BlockSpec((1, rows, w), lambda b, c: (b, c, 0))
    vec = lambda w: _const_spec((1, w))
    state = (HEAD, RWKV_WIDTH)
    r = np.arange(rows)
    tri = jnp.asarray((r[:, None] // CHUNK == r[None, :] // CHUNK) & (r[None, :] <= r[:, None]), BF16)
    return pl.pallas_call(
        functools.partial(_rwkv_kernel, nc=nc),
        grid=(B, T // rows),
        in_specs=[blk(RKV_COLS), blk(LORA_PAD), _const_spec(state), _const_spec((CHUNK, RKV_COLS)),
                  _const_spec((CHUNK, LORA_PAD)),
                  vec(RKV_COLS), vec(LORA_PAD), _const_spec((LORA_PAD, 3 * RWKV_WIDTH))]
                 + [vec(RWKV_WIDTH)] * 7 + [_const_spec((QUAD, QUAD)), _const_spec((rows, rows))],
        out_specs=[blk(RWKV_WIDTH), pl.BlockSpec((1,) + state, lambda b, c: (b, 0, 0))],
        out_shape=[jax.ShapeDtypeStruct((B, T, RWKV_WIDTH), BF16),
                   jax.ShapeDtypeStruct((B,) + state, F32)],
        scratch_shapes=[pltpu.VMEM(state, F32), pltpu.VMEM((1, RKV_COLS), F32),
                        pltpu.VMEM((1, LORA_PAD), F32)],
        compiler_params=pltpu.CompilerParams(dimension_semantics=("arbitrary", "arbitrary"),
                                             vmem_limit_bytes=VMEM_LIMIT),
        name="rwkv",
    )(zrkv, zlora, s0, prev_rkv, prev_lora, p["mu_rkv"], p["mu_lora"], p["wlora"], p["w0"], p["a0"],
      p["k_k"], p["k_a"], p["r_k"], p["ln_w"], p["ln_b"], p["bd"], tri)


N_WIN = (WIN_CHUNKS + 1) * CHUNK
KEY_SLOTS = 256
ATTN_CHUNKS_PER_STEP = 8
PAIR = 2 * HEAD
LOGIT_SCALE = HEAD ** -0.5 * LOG2E
ATTN_HEAD_ORDER = tuple(h for p in range(ATTN_GROUP) for h in (p, ATTN_GROUP + p))


def _attn_bias():
    p = np.arange(ATTN_GROUP)[:, None, None, None]
    i = np.arange(CHUNK)[None, :, None, None]
    j = np.arange(ATTN_KV_HEADS)[None, None, :, None]
    s = np.arange(KEY_SLOTS)[None, None, None, :]
    slope = np.exp2(-8.0 * (ATTN_GROUP * j + p + 1.0) / ATTN_HEADS)
    meta = s >= KEY_SLOTS - N_META
    dist = np.where(meta, N_META + i - (s - (KEY_SLOTS - N_META)), np.abs(WIN_CHUNKS * CHUNK + i - s))
    b0 = np.where((s >= N_WIN) & ~meta, NEG_INF, -slope * dist)
    b1 = np.broadcast_to(np.where(meta, slope * CHUNK, 0.0), b0.shape)
    shape = (ATTN_GROUP * CHUNK, ATTN_KV_HEADS * KEY_SLOTS)
    return (jnp.asarray(LOG2E * b0.reshape(shape), F32), jnp.asarray(LOG2E * b1.reshape(shape), F32))


def _attn_kernel(sink_ref, q_ref, kvp_ref, kvc_ref, kvm_ref, b0_ref, b1_ref, o_ref):
    step = pl.program_id(1)
    nchunks = ATTN_CHUNKS_PER_STEP
    rows = ATTN_GROUP * CHUNK
    kv_rows = jnp.concatenate([kvp_ref[0], kvc_ref[0]], axis=0)
    kvm = kvm_ref[...]
    first_head = lax.broadcasted_iota(jnp.int32, (1, PAIR), 1) < HEAD
    slot = lax.broadcasted_iota(jnp.int32, (1, ATTN_KV_HEADS * KEY_SLOTS), 1) % KEY_SLOTS
    pair_of_row = lax.broadcasted_iota(jnp.int32, (rows, 1), 0) // CHUNK
    sinks = []
    for j in range(ATTN_KV_HEADS):
        col = jnp.zeros((rows, 1), F32)
        for p in range(ATTN_GROUP):
            col = jnp.where(pair_of_row == p, sink_ref[ATTN_GROUP * j + p], col)
        sinks.append(col)

    def per_head(x):
        zero = jnp.zeros_like(x)
        return jnp.concatenate([jnp.where(first_head, x, zero), jnp.where(first_head, zero, x)], axis=0)

    chunk_ids = [step * nchunks + n for n in range(nchunks)]
    keys = [jnp.concatenate([kv_rows[n * CHUNK:n * CHUNK + N_WIN], kvm], axis=0) for n in range(nchunks)]
    qs = [jnp.concatenate([q_ref[0, n * CHUNK:(n + 1) * CHUNK, p * PAIR:(p + 1) * PAIR]
                           for p in range(ATTN_GROUP)], axis=0) for n in range(nchunks)]
    s = [lax.dot_general(q, per_head(kx[:, :PAIR]), (((1,), (1,)), ((), ())), preferred_element_type=F32)
         for q, kx in zip(qs, keys)]
    def add_bias(x, c):
        half = KEY_SLOTS // 2
        parts = []
        for j in range(ATTN_KV_HEADS):
            near = slice(j * KEY_SLOTS, j * KEY_SLOTS + half)
            far = slice(j * KEY_SLOTS + half, (j + 1) * KEY_SLOTS)
            parts.append(x[:, near] + b0_ref[:, near])
            parts.append(x[:, far] + (b0_ref[:, far] - c.astype(F32) * b1_ref[:, far]))
        return jnp.concatenate(parts, axis=1)

    s = [add_bias(x, c) for x, c in zip(s, chunk_ids)]
    s = [jnp.where(slot >= (WIN_CHUNKS - c) * CHUNK, x, NEG_INF) if n < WIN_CHUNKS else x
         for n, (x, c) in enumerate(zip(s, chunk_ids))]
    outs = []
    for n in range(nchunks):
        ps, inv_den = [], []
        for j in range(ATTN_KV_HEADS):
            sj = s[n][:, j * KEY_SLOTS:(j + 1) * KEY_SLOTS]
            m = jnp.maximum(jnp.max(sj, axis=-1, keepdims=True), sinks[j])
            pj = jnp.exp2(sj - m)
            inv_den.append(1.0 / (jnp.sum(pj, axis=-1, keepdims=True) + jnp.exp2(sinks[j] - m)))
            ps.append(pj.astype(BF16))
        o = jnp.dot(jnp.concatenate(ps, axis=1), per_head(keys[n][:, PAIR:]),
                    preferred_element_type=F32)
        o = o * jnp.where(first_head, inv_den[0], inv_den[1])
        outs.append(jnp.concatenate([o[p * CHUNK:(p + 1) * CHUNK] for p in range(ATTN_GROUP)], axis=1))
    o_ref[0] = jnp.concatenate(outs, axis=0).astype(o_ref.dtype)


def _attn(q, kv, kv_meta, sinks):
    B, T, _ = q.shape
    rows = ATTN_CHUNKS_PER_STEP * CHUNK
    back = WIN_CHUNKS * CHUNK
    b0, b1 = _attn_bias()
    return pl.pallas_call(
        _attn_kernel,
        grid=(B, T // rows),
        in_specs=[pl.BlockSpec(memory_space=pltpu.SMEM),
                  pl.BlockSpec((1, rows, ATTN_WIDTH), lambda b, c: (b, c, 0)),
                  pl.BlockSpec((1, back, KV_WIDTH),
                               lambda b, c: (b, jnp.maximum(c * (rows // back) - 1, 0), 0)),
                  pl.BlockSpec((1, rows, KV_WIDTH), lambda b, c: (b, c, 0)),
                  _const_spec((CHUNK, KV_WIDTH)), _const_spec(b0.shape), _const_spec(b1.shape)],
        out_specs=pl.BlockSpec((1, rows, ATTN_WIDTH), lambda b, c: (b, c, 0)),
        out_shape=jax.ShapeDtypeStruct((B, T, ATTN_WIDTH), BF16),
        compiler_params=pltpu.CompilerParams(dimension_semantics=("arbitrary", "arbitrary"),
                                             vmem_limit_bytes=VMEM_LIMIT),
        name="attn",
    )(sinks, q, kv, kv, kv_meta, b0, b1)


def _out_ffn_kernel(h_ref, yr_ref, ya_ref, wo_ref, g_ref, wg_ref, wu_ref, wd_ref, gf_ref, o_ref):
    y = jnp.concatenate([yr_ref[...], ya_ref[...]], axis=-1)
    h = h_ref[...] + jnp.dot(y, wo_ref[...], preferred_element_type=F32)
    n = _rmsnorm(h, g_ref[...]).astype(BF16)
    h = h + 0.5 * _swiglu(n, wg_ref, wu_ref, wd_ref)
    o_ref[...] = _rmsnorm(h, gf_ref[...])


def _out_ffn(h, y_rwkv, y_attn, wo, g, wg, wu, wd, gf, tm):
    rows = h.shape[0]
    row = lambda w: pl.BlockSpec((tm, w), lambda i: (i, 0))
    return pl.pallas_call(
        _out_ffn_kernel,
        grid=(rows // tm,),
        in_specs=[row(D_MODEL), row(RWKV_WIDTH), row(ATTN_WIDTH), _const_spec((D_MODEL, D_MODEL)),
                  _const_spec((1, D_MODEL)), _const_spec((D_MODEL, D_FF)), _const_spec((D_MODEL, D_FF)),
                  _const_spec((D_FF, D_MODEL)), _const_spec((1, D_MODEL))],
        out_specs=row(D_MODEL),
        out_shape=jax.ShapeDtypeStruct((rows, D_MODEL), F32),
        compiler_params=pltpu.CompilerParams(dimension_semantics=("arbitrary",),
                                             vmem_limit_bytes=VMEM_LIMIT),
        name="out_ffn",
    )(h, y_rwkv, y_attn, wo, g, wg, wu, wd, gf)


def _head_cols(order):
    return np.concatenate([np.arange(h * HEAD, (h + 1) * HEAD) for h in order])


def _pack_w_in(w_in, b_attn):
    rwkv_cols = RKV_COLS + LORA_COLS
    pad = jnp.zeros((D_MODEL, LORA_PAD - LORA_COLS), BF16)
    attn_cols = np.concatenate([_head_cols(ATTN_HEAD_ORDER), np.arange(ATTN_WIDTH, ATTN_WIDTH + KV_WIDTH)])
    w_in = w_in.astype(BF16)
    w = jnp.concatenate([w_in[:, :rwkv_cols], pad, w_in[:, rwkv_cols + attn_cols]], axis=1)
    b = jnp.concatenate([jnp.zeros((RKV_COLS + LORA_PAD,), F32), b_attn.astype(F32)[attn_cols]])[None]
    return w, b


def _pack_lora(w2, a2, g2):
    w = jnp.zeros((LORA_PAD, 3 * RWKV_WIDTH), F32)
    w = w.at[0:LORA_W, 0:RWKV_WIDTH].set(w2)
    w = w.at[LORA_W:LORA_W + LORA_A, RWKV_WIDTH:2 * RWKV_WIDTH].set(a2)
    w = w.at[LORA_W + LORA_A:LORA_COLS, 2 * RWKV_WIDTH:].set(g2)
    return w


def _block_ones():
    i = jnp.arange(QUAD) // HEAD
    return (i[:, None] == i[None, :]).astype(BF16)


def kernel(x, meta_tokens, ffn1_norm, ffn1_w_gate, ffn1_w_up, ffn1_w_down, mix_norm, w_in, b_attn, rwkv_mu, rwkv_w0, rwkv_w2, rwkv_a0, rwkv_a2, rwkv_g2, rwkv_k_k, rwkv_k_a, rwkv_r_k, rwkv_ln_w, rwkv_ln_b, attn_sinks, w_out, ffn2_norm, ffn2_w_gate, ffn2_w_up, ffn2_w_down, final_norm):
    assert ffn1_norm.shape[0] == 1, "single-layer trunk"
    B, T, D = x.shape
    row = lambda v: v.reshape(1, -1).astype(F32)
    bf = lambda w: w[0].astype(BF16)

    win, b_in = _pack_w_in(w_in[0], b_attn[0])
    mu = rwkv_mu[0].astype(F32)
    mu_lora = jnp.concatenate([mu[RKV_COLS:], jnp.zeros((LORA_PAD - LORA_COLS,), F32)])
    rp = dict(mu_rkv=row(mu[:RKV_COLS]), mu_lora=row(mu_lora),
              wlora=_pack_lora(rwkv_w2[0], rwkv_a2[0], rwkv_g2[0]).astype(BF16), bd=_block_ones(),
              w0=row(rwkv_w0[0]), a0=row(rwkv_a0[0]), k_k=row(rwkv_k_k[0]), k_a=row(rwkv_k_a[0]),
              r_k=row(rwkv_r_k[0]), ln_w=row(rwkv_ln_w[0]), ln_b=row(rwkv_ln_b[0]))
    ffn1 = (row(ffn1_norm[0]), bf(ffn1_w_gate), bf(ffn1_w_up), bf(ffn1_w_down))

    xm = jnp.concatenate([jnp.zeros((CHUNK - N_META, D), F32), meta_tokens.astype(F32)], axis=0)
    _, zrkv_m, zlora_m, _, kv_m = _ffn_in(xm, *ffn1, row(mix_norm[0]), win, b_in, tm=CHUNK)
    _, s_meta = _rwkv(zrkv_m[None], zlora_m[None], jnp.zeros((HEAD, RWKV_WIDTH), F32),
                      jnp.zeros((CHUNK, RKV_COLS), F32), jnp.zeros((CHUNK, LORA_PAD), F32), rp, nc=1)

    xf = x.reshape(B * T, D)
    h1, zrkv, zlora, q, kv = _ffn_in(xf, *ffn1, row(mix_norm[0]), win, b_in, tm=FFN_ROWS)
    y_rwkv, _ = _rwkv(zrkv.reshape(B, T, -1), zlora.reshape(B, T, -1), s_meta[0], zrkv_m, zlora_m, rp,
                      nc=RWKV_CHUNKS_PER_STEP)
    y_attn = _attn(q.reshape(B, T, -1), kv.reshape(B, T, -1), kv_m, attn_sinks[0].astype(F32) * LOG2E)
    wo_rows = np.concatenate([np.arange(RWKV_WIDTH), RWKV_WIDTH + _head_cols(ATTN_HEAD_ORDER)])
    out = _out_ffn(h1, y_rwkv.reshape(B * T, -1), y_attn.reshape(B * T, -1), w_out[0].astype(BF16)[wo_rows],
                   row(ffn2_norm[0]), bf(ffn2_w_gate), bf(ffn2_w_up), bf(ffn2_w_down),
                   row(final_norm), tm=OUT_FFN_ROWS)
    return out.reshape(B, T, D)
```
